```python
import jax, jax.numpy as jnp
from jax import lax
import numpy as np

D_MODEL = 1024
BATCH = 8
SEQ = 2048
DEPTH = 1

CHUNK = 64
CONV_WIDTH = 4
NORM_EPS = 1e-5
SSD_INNER = 2 * D_MODEL
SSD_HEAD_DIM = 64
SSD_HEADS = SSD_INNER // SSD_HEAD_DIM
SSD_GROUPS = 4
SSD_HEADS_PER_GROUP = SSD_HEADS // SSD_GROUPS
SSD_STATE = 128
SSD_XBC = SSD_INNER + 2 * SSD_GROUPS * SSD_STATE
MLSTM_INNER = D_MODEL
MLSTM_HEADS = 4
MLSTM_HEAD_DIM = MLSTM_INNER // MLSTM_HEADS
D_FF = 4 * D_MODEL
SPLIT_SIZES = (SSD_INNER, SSD_XBC, SSD_HEADS,
               MLSTM_INNER, MLSTM_INNER, MLSTM_INNER, MLSTM_INNER,
               MLSTM_HEADS, MLSTM_HEADS, 2 * D_MODEL)
IN_PROJ_WIDTH = sum(SPLIT_SIZES)

kernel_name = 'hybrid_ssd_mlstm_gated_block'


def rms_norm(x, w):
    xf = x.astype(jnp.float32)
    xf = xf * lax.rsqrt(jnp.mean(xf * xf, axis=-1, keepdims=True) + NORM_EPS)
    return (xf * w.astype(jnp.float32)).astype(x.dtype)


def causal_dwconv(u, w, b):
    S = u.shape[1]
    K = w.shape[0]
    up = jnp.pad(u, ((0, 0), (K - 1, 0), (0, 0)))
    out = b
    for tap in range(K):
        out = out + up[:, tap:tap + S] * w[tap]
    return out


def ssd_chunked(xs, dt, A, Bm, Cm):
    Bsz, S, G, R, P = xs.shape
    N = Bm.shape[-1]
    nc, L = S // CHUNK, CHUNK
    xc = xs.reshape(Bsz, nc, L, G, R, P)
    dtc = dt.reshape(Bsz, nc, L, G, R)
    Bc = Bm.reshape(Bsz, nc, L, G, N)
    Cc = Cm.reshape(Bsz, nc, L, G, N)
    acum = jnp.cumsum(dtc * A, axis=2)
    causal = jnp.tril(jnp.ones((L, L), dtype=bool))
    seg = acum[:, :, :, None] - acum[:, :, None]
    decay = jnp.exp(jnp.where(causal[:, :, None, None], seg, -jnp.inf))
    cb = jnp.einsum('bctgn,bcsgn->bctsg', Cc, Bc)
    w = cb[..., None] * decay * dtc[:, :, None]
    y_intra = jnp.einsum('bctsgr,bcsgrp->bctgrp', w, xc)
    to_end = jnp.exp(acum[:, :, -1:] - acum) * dtc
    states = jnp.einsum('bclgn,bclgr,bclgrp->bcgrpn', Bc, to_end, xc)
    chunk_decay = jnp.exp(acum[:, :, -1])

    def step(hst, inp):
        s, dcy = inp
        return hst * dcy[..., None, None] + s, hst

    h0 = jnp.zeros((Bsz, G, R, P, N), jnp.float32)
    _, h_prev = lax.scan(step, h0, (jnp.moveaxis(states, 1, 0), jnp.moveaxis(chunk_decay, 1, 0)))
    h_prev = jnp.moveaxis(h_prev, 0, 1)
    y_inter = jnp.einsum('bctgn,bcgrpn->bctgrp', Cc, h_prev) * jnp.exp(acum)[..., None]
    return (y_intra + y_inter).reshape(Bsz, S, G, R, P)


def mlstm_chunkwise(q, k, v, i_pre, f_pre):
    Bsz, S, H, dh = q.shape
    nc, L = S // CHUNK, CHUNK
    qc = (q * dh ** -0.5).reshape(Bsz, nc, L, H, dh)
    kc = k.reshape(Bsz, nc, L, H, dh)
    vc = v.reshape(Bsz, nc, L, H, dh)
    li = i_pre.reshape(Bsz, nc, L, H)
    lf = jax.nn.log_sigmoid(f_pre).reshape(Bsz, nc, L, H)
    b = jnp.cumsum(lf, axis=2)
    b_end = b[:, :, -1]
    causal = jnp.tril(jnp.ones((L, L), dtype=bool))
    d_log = b[:, :, :, None] - b[:, :, None] + li[:, :, None]
    d_log = jnp.where(causal[:, :, None], d_log, -jnp.inf)
    g = b_end[:, :, None] - b + li
    m_loc = jnp.max(g, axis=2)
    wg = jnp.exp(g - m_loc[:, :, None])
    c_loc = jnp.einsum('bclh,bclhk,bclhv->bchkv', wg, kc, vc)
    n_loc = jnp.einsum('bclh,bclhk->bchk', wg, kc)

    def step(carry, inp):
        c_st, n_st, m_st = carry
        cl, nl, ml, bl = inp
        m_new = jnp.maximum(bl + m_st, ml)
        a_old = jnp.exp(bl + m_st - m_new)
        a_loc = jnp.exp(ml - m_new)
        c_new = a_old[..., None, None] * c_st + a_loc[..., None, None] * cl
        n_new = a_old[..., None] * n_st + a_loc[..., None] * nl
        return (c_new, n_new, m_new), (c_st, n_st, m_st)

    init = (jnp.zeros((Bsz, H, dh, dh), jnp.float32),
            jnp.zeros((Bsz, H, dh), jnp.float32),
            jnp.zeros((Bsz, H), jnp.float32))
    xs_in = tuple(jnp.moveaxis(t, 1, 0) for t in (c_loc, n_loc, m_loc, b_end))
    _, (c_prev, n_prev, m_prev) = lax.scan(step, init, xs_in)
    c_prev = jnp.moveaxis(c_prev, 0, 1)
    n_prev = jnp.moveaxis(n_prev, 0, 1)
    m_prev = jnp.moveaxis(m_prev, 0, 1)
    inter_log = b + m_prev[:, :, None]
    m_t = jnp.maximum(inter_log, jnp.max(d_log, axis=3))
    w_inter = jnp.exp(inter_log - m_t)
    w_intra = jnp.exp(d_log - m_t[:, :, :, None])
    qk = jnp.einsum('bcthk,bcshk->bctsh', qc, kc) * w_intra
    num = (jnp.einsum('bctsh,bcshv->bcthv', qk, vc)
           + w_inter[..., None] * jnp.einsum('bcthk,bchkv->bcthv', qc, c_prev))
    den = jnp.sum(qk, axis=3) + w_inter * jnp.einsum('bcthk,bchk->bcth', qc, n_prev)
    h = num / jnp.maximum(jnp.abs(den), jnp.exp(-m_t))[..., None]
    return h.reshape(Bsz, S, H, dh)


def setup_inputs(seed: int = 0) -> dict:
    key = jax.random.key(seed)
    ks = jax.random.split(key, 24)
    f32 = jnp.float32

    def nrm(k, shape, scale):
        return jax.random.normal(k, shape, f32) * scale

    dt0 = jnp.exp(jax.random.uniform(ks[5], (DEPTH, SSD_HEADS), f32, np.log(1e-3), np.log(1e-1)))
    return {
        'x': nrm(ks[0], (BATCH, SEQ, D_MODEL), 1.0),
        'norm_mix_w': 1.0 + nrm(ks[1], (DEPTH, D_MODEL), 0.02),
        'w_in': nrm(ks[2], (DEPTH, D_MODEL, IN_PROJ_WIDTH), D_MODEL ** -0.5),
        'conv_ssd_w': nrm(ks[3], (DEPTH, CONV_WIDTH, SSD_XBC), CONV_WIDTH ** -0.5),
        'conv_ssd_b': nrm(ks[4], (DEPTH, SSD_XBC), 0.01),
        'dt_bias': dt0 + jnp.log(-jnp.expm1(-dt0)),
        'a_log': jnp.log(jax.random.uniform(ks[6], (DEPTH, SSD_HEADS), f32, 1.0, 16.0)),
        'd_skip': 1.0 + nrm(ks[7], (DEPTH, SSD_HEADS), 0.1),
        'ssd_norm_w': 1.0 + nrm(ks[8], (DEPTH, SSD_INNER), 0.02),
        'conv_qk_w': nrm(ks[9], (DEPTH, CONV_WIDTH, 2 * MLSTM_INNER), CONV_WIDTH ** -0.5),
        'conv_qk_b': nrm(ks[10], (DEPTH, 2 * MLSTM_INNER), 0.01),
        'i_bias': nrm(ks[11], (DEPTH, MLSTM_HEADS), 0.1),
        'f_bias': jnp.linspace(3.0, 6.0, MLSTM_HEADS, dtype=f32)[None] + nrm(ks[12], (DEPTH, MLSTM_HEADS), 0.1),
        'mlstm_norm_w': 1.0 + nrm(ks[13], (DEPTH, MLSTM_INNER), 0.02),
        'w_br_ssd': nrm(ks[14], (DEPTH, SSD_INNER, D_MODEL), SSD_INNER ** -0.5),
        'w_br_mlstm': nrm(ks[15], (DEPTH, MLSTM_INNER, D_MODEL), MLSTM_INNER ** -0.5),
        'w_out': nrm(ks[16], (DEPTH, D_MODEL, D_MODEL), D_MODEL ** -0.5),
        'norm_mlp_w': 1.0 + nrm(ks[17], (DEPTH, D_MODEL), 0.02),
        'w_up': nrm(ks[18], (DEPTH, D_MODEL, D_FF), D_MODEL ** -0.5),
        'w_down': nrm(ks[19], (DEPTH, D_FF, D_MODEL), D_FF ** -0.5),
        'norm_final_w': 1.0 + nrm(ks[20], (D_MODEL,), 0.02),
    }


def reference(x, norm_mix_w, w_in, conv_ssd_w, conv_ssd_b, dt_bias, a_log, d_skip, ssd_norm_w,
              conv_qk_w, conv_qk_b, i_bias, f_bias, mlstm_norm_w, w_br_ssd, w_br_mlstm, w_out,
              norm_mlp_w, w_up, w_down, norm_final_w):
    f32 = jnp.float32
    Bsz, S, _ = x.shape
    G, R, P, N = SSD_GROUPS, SSD_HEADS_PER_GROUP, SSD_HEAD_DIM, SSD_STATE
    H, dh = MLSTM_HEADS, MLSTM_HEAD_DIM
    offsets = [int(o) for o in np.cumsum(SPLIT_SIZES)[:-1]]
    h = x
    for layer in range(DEPTH):
        u = rms_norm(h, norm_mix_w[layer])
        proj = u @ w_in[layer]
        z, xbc, dt_raw, q, k, v, o, i_pre, f_pre, gates = jnp.split(proj, offsets, axis=-1)

        xbc = jax.nn.silu(causal_dwconv(xbc, conv_ssd_w[layer], conv_ssd_b[layer]))
        xs, bm, cm = jnp.split(xbc, [SSD_INNER, SSD_INNER + G * N], axis=-1)
        dt = jax.nn.softplus(dt_raw.astype(f32) + dt_bias[layer].astype(f32))
        A = -jnp.exp(a_log[layer].astype(f32))
        xs_h = xs.astype(f32).reshape(Bsz, S, G, R, P)
        y = ssd_chunked(xs_h, dt.reshape(Bsz, S, G, R), A.reshape(G, R),
                        bm.astype(f32).reshape(Bsz, S, G, N), cm.astype(f32).reshape(Bsz, S, G, N))
        y = y + d_skip[layer].astype(f32).reshape(G, R)[:, :, None] * xs_h
        y = y.reshape(Bsz, S, SSD_INNER) * jax.nn.silu(z.astype(f32))
        y = rms_norm(y.reshape(Bsz, S, G, SSD_INNER // G),
                     ssd_norm_w[layer].reshape(G, SSD_INNER // G)).reshape(Bsz, S, SSD_INNER)

        qk = jax.nn.silu(causal_dwconv(jnp.concatenate([q, k], axis=-1), conv_qk_w[layer], conv_qk_b[layer]))
        q_c, k_c = jnp.split(qk, 2, axis=-1)
        hm = mlstm_chunkwise(q_c.astype(f32).reshape(Bsz, S, H, dh), k_c.astype(f32).reshape(Bsz, S, H, dh),
                             v.astype(f32).reshape(Bsz, S, H, dh),
                             i_pre.astype(f32) + i_bias[layer].astype(f32),
                             f_pre.astype(f32) + f_bias[layer].astype(f32))
        hm = rms_norm(hm, mlstm_norm_w[layer].reshape(H, dh)).reshape(Bsz, S, MLSTM_INNER)
        hm = jax.nn.sigmoid(o.astype(f32)) * hm

        g_ssd, g_ml = jnp.split(gates, 2, axis=-1)
        mixed = (jax.nn.sigmoid(g_ssd) * (y.astype(h.dtype) @ w_br_ssd[layer])
                 + jax.nn.sigmoid(g_ml) * (hm.astype(h.dtype) @ w_br_mlstm[layer]))
        h = h + mixed @ w_out[layer]

        u = rms_norm(h, norm_mlp_w[layer])
        h = h + jnp.square(jax.nn.relu(u @ w_up[layer])) @ w_down[layer]
    return rms_norm(h, norm_final_w)
```

```python
import functools

import jax
import jax.numpy as jnp
from jax import lax
from jax.experimental import pallas as pl
from jax.experimental.pallas import tpu as pltpu

F32 = jnp.float32
BF16 = jnp.bfloat16

D_MODEL = 1024
NORM_EPS = 1e-5
CONV_TAPS = 4
SSD_INNER = 2048
SSD_HEAD_DIM = 64
SSD_HEADS = 32
SSD_GROUPS = 4
SSD_STATE = 128
ML_INNER = 1024
ML_HEADS = 4
ML_HEAD_DIM = 256
D_FF = 4096

LANES = 128
SUBLANES = 8

COL_BLOCK = 1024
MAIN_WIDTH = 11 * COL_BLOCK
SMALL_WIDTH = LANES
DT_COL, I_COL, F_COL = 0, 32, 36
DT_ROW_COPY = 64

SCAN_CHUNK = 128
MIX_TOKENS = 256
PROJ_TM = 1024
MERGE_TM = 512
HEADS_PER_DOT = 4
VMEM_LIMIT = 56 * 1024 * 1024


def _softplus(x):
    return jnp.maximum(x, 0.0) + jnp.log1p(jnp.exp(-jnp.abs(x)))


def _sigmoid(x):
    return 1.0 / (1.0 + jnp.exp(-x))


def _silu(x):
    return x * _sigmoid(x)


def _dot(a, b):
    return jnp.dot(a, b, preferred_element_type=F32)


def _dot_nt(a, b):
    return lax.dot_general(a, b, (((1,), (1,)), ((), ())), preferred_element_type=F32)


def _dot_tn(a, b):
    return lax.dot_general(a, b, (((0,), (0,)), ((), ())), preferred_element_type=F32)


def _in_proj_kernel(x_ref, nw_ref, w_ref, ws_ref, main_ref, small_ref, u_ref):
    j = pl.program_id(1)

    @pl.when(j == 0)
    def _():
        x = x_ref[...]
        ms = jnp.mean(x * x, axis=-1, keepdims=True)
        u = (x * lax.rsqrt(ms + NORM_EPS) * nw_ref[...]).astype(BF16)
        u_ref[...] = u
        small_ref[...] = _dot(u, ws_ref[...])

    main_ref[...] = _dot(u_ref[...], w_ref[...]).astype(BF16)


def _in_proj(x2, norm_w, w_main, w_small):
    tokens = x2.shape[0]
    grid = (tokens // PROJ_TM, MAIN_WIDTH // COL_BLOCK)
    return pl.pallas_call(
        _in_proj_kernel,
        grid=grid,
        in_specs=[
            pl.BlockSpec((PROJ_TM, D_MODEL), lambda i, j: (i, 0)),
            pl.BlockSpec((1, D_MODEL), lambda i, j: (0, 0)),
            pl.BlockSpec((D_MODEL, COL_BLOCK), lambda i, j: (0, j)),
            pl.BlockSpec((D_MODEL, SMALL_WIDTH), lambda i, j: (0, 0)),
        ],
        out_specs=[
            pl.BlockSpec((PROJ_TM, COL_BLOCK), lambda i, j: (i, j)),
            pl.BlockSpec((PROJ_TM, SMALL_WIDTH), lambda i, j: (i, 0)),
        ],
        out_shape=[
            jax.ShapeDtypeStruct((tokens, MAIN_WIDTH), BF16),
            jax.ShapeDtypeStruct((tokens, SMALL_WIDTH), F32),
        ],
        scratch_shapes=[pltpu.VMEM((PROJ_TM, D_MODEL), BF16)],
        compiler_params=pltpu.CompilerParams(
            dimension_semantics=("arbitrary", "arbitrary"),
            vmem_limit_bytes=VMEM_LIMIT),
        name="in_proj",
    )(x2, norm_w, w_main, w_small)


def _conv_silu(src_ref, ext_ref, w_ref, b_ref, dst_ref, width):
    tb = MIX_TOKENS
    strip = 512
    for c0 in range(0, width, strip):
        cols = slice(c0, c0 + strip)
        ext_ref[SUBLANES:SUBLANES + tb, cols] = src_ref[:, cols].astype(F32)
        acc = b_ref[:, cols] + ext_ref[SUBLANES:SUBLANES + tb, cols] * w_ref[3:4, cols]
        for tap in range(CONV_TAPS - 1):
            shift = CONV_TAPS - 1 - tap
            acc = acc + ext_ref[SUBLANES - shift:SUBLANES - shift + tb, cols] * w_ref[tap:tap + 1, cols]
        dst_ref[:, cols] = _silu(acc).astype(BF16)
        ext_ref[0:SUBLANES, cols] = ext_ref[tb:tb + SUBLANES, cols]


def _mixer_kernel(z_ref, xs_ref, bc_ref, q_ref, k_ref, v_ref, o_ref, sm_ref,
                  cw_xs_ref, cb_xs_ref, cw_bc_ref, cb_bc_ref, cw_q_ref, cb_q_ref, cw_k_ref, cb_k_ref,
                  bias_ref, alog_ref, dskip_ref, ssd_nw_ref, ml_nw_ref,
                  y_ref, hm_ref,
                  ext_xs, ext_bc, ext_q, ext_k, xs_c, bc_c, q_c, k_c,
                  s_ref, c_ref, n_ref, m_ref, p_ref, pt_ref, dt_ref, ybuf):
    tb, L = MIX_TOKENS, SCAN_CHUNK
    n_chunks = tb // L
    hp = HEADS_PER_DOT
    qw = hp * SSD_HEAD_DIM

    @pl.when(pl.program_id(1) == 0)
    def _():
        for ext in (ext_xs, ext_bc, ext_q, ext_k):
            ext[0:SUBLANES, :] = jnp.zeros((SUBLANES, ext.shape[1]), F32)
        s_ref[...] = jnp.zeros_like(s_ref)
        c_ref[...] = jnp.zeros_like(c_ref)
        n_ref[...] = jnp.zeros_like(n_ref)
        m_ref[...] = jnp.zeros_like(m_ref)

    _conv_silu(xs_ref, ext_xs, cw_xs_ref, cb_xs_ref, xs_c, SSD_INNER)
    _conv_silu(bc_ref, ext_bc, cw_bc_ref, cb_bc_ref, bc_c, 2 * SSD_GROUPS * SSD_STATE)
    _conv_silu(q_ref, ext_q, cw_q_ref, cb_q_ref, q_c, ML_INNER)
    _conv_silu(k_ref, ext_k, cw_k_ref, cb_k_ref, k_c, ML_INNER)

    sm = sm_ref[...] + bias_ref[...]
    lane = lax.broadcasted_iota(jnp.int32, (tb, SMALL_WIDTH), 1)
    dt = _softplus(sm)
    log_f = -_softplus(-sm)
    a_row = -jnp.exp(alog_ref[...])
    is_dt = lane < I_COL
    is_i = (lane >= I_COL) & (lane < F_COL)
    is_f = (lane >= F_COL) & (lane < F_COL + ML_HEADS)
    pre = jnp.where(is_dt, dt * a_row, jnp.where(is_f, log_f, 0.0))
    rt = lax.broadcasted_iota(jnp.int32, (tb, tb), 0)
    rs = lax.broadcasted_iota(jnp.int32, (tb, tb), 1)
    tri = ((rs <= rt) & ((rt // L) == (rs // L))).astype(F32)
    cs = jnp.dot(tri, pre, preferred_element_type=F32, precision=lax.Precision.HIGHEST)
    dt_copy = pltpu.roll(dt, DT_ROW_COPY, axis=1)
    is_dt_copy = (lane >= DT_ROW_COPY) & (lane < DT_ROW_COPY + SSD_HEADS)
    table = jnp.where(is_dt | is_f, cs, jnp.where(is_i, sm, jnp.where(is_dt_copy, dt_copy, 0.0)))
    p_ref[...] = table
    dt_ref[...] = dt
    for c in range(n_chunks):
        pt_ref[c] = table[c * L:(c + 1) * L, :].T

    causal = (lax.broadcasted_iota(jnp.int32, (L, L), 0) >= lax.broadcasted_iota(jnp.int32, (L, L), 1))
    lane_q = lax.broadcasted_iota(jnp.int32, (1, qw), 1)
    head_masks = [(lane_q >= a * SSD_HEAD_DIM) & (lane_q < (a + 1) * SSD_HEAD_DIM) for a in range(hp)]
    neg_inf = jnp.float32(-jnp.inf)

    def chunk_body(c, carry):
        r0 = pl.multiple_of(c * L, L)
        rows = pl.ds(r0, L)
        tab = p_ref[rows, :]
        tab_end = p_ref[pl.ds(r0 + L - 1, 1), :]
        tab_t = pt_ref[c]
        dt_blk = dt_ref[rows, :]
        e_acum = jnp.exp(tab)
        e_end = jnp.exp(tab_end)
        to_end_dt = jnp.exp(tab_end - tab) * dt_blk

        for qd in range(SSD_HEADS // hp):
            g = (qd * hp) // (SSD_HEADS // SSD_GROUPS)
            b_mat = bc_c[rows, g * SSD_STATE:(g + 1) * SSD_STATE]
            c_mat = bc_c[rows, (SSD_GROUPS + g) * SSD_STATE:(SSD_GROUPS + g + 1) * SSD_STATE]
            cb = _dot_nt(c_mat, b_mat)
            b_f32 = b_mat.astype(F32)
            c_f32 = c_mat.astype(F32)
            cols = slice(qd * qw, (qd + 1) * qw)
            x_q = xs_c[rows, cols]
            s_q = s_ref[:, cols]
            s_q16 = s_q.astype(BF16)
            zero_x = jnp.zeros_like(x_q)
            zero_s = jnp.zeros_like(s_q16)
            lhs_parts, rhs_parts, bs_parts = [], [], []
            decay_row = jnp.zeros((1, qw), F32)
            for a in range(hp):
                h = qd * hp + a
                seg = tab[:, h:h + 1] - tab_t[h:h + 1, :]
                dec = jnp.exp(jnp.where(causal, seg, neg_inf))
                w_mat = cb * dec * tab_t[DT_ROW_COPY + h:DT_ROW_COPY + h + 1, :]
                lhs_parts.append(w_mat.astype(BF16))
                lhs_parts.append((c_f32 * e_acum[:, h:h + 1]).astype(BF16))
                rhs_parts.append(jnp.where(head_masks[a], x_q, zero_x))
                rhs_parts.append(jnp.where(head_masks[a], s_q16, zero_s))
                bs_parts.append((b_f32 * to_end_dt[:, h:h + 1]).astype(BF16))
                decay_row = jnp.where(head_masks[a], e_end[:, h:h + 1], decay_row)
            lhs = jnp.concatenate(lhs_parts, axis=1)
            rhs = jnp.concatenate(rhs_parts, axis=0)
            y_q = _dot(lhs, rhs) + dskip_ref[:, cols] * x_q.astype(F32)
            ybuf[:, cols] = y_q
            upd = _dot_tn(jnp.concatenate(bs_parts, axis=1), x_q)
            new = jnp.zeros((SSD_STATE, qw), F32)
            for a in range(hp):
                new = jnp.where(head_masks[a], upd[a * SSD_STATE:(a + 1) * SSD_STATE, :], new)
            s_ref[:, cols] = s_q * decay_row + new

        gw = SSD_INNER // SSD_GROUPS
        for g in range(SSD_GROUPS):
            cols = slice(g * gw, (g + 1) * gw)
            yz = ybuf[:, cols] * _silu(z_ref[rows, cols].astype(F32))
            ms = jnp.mean(yz * yz, axis=-1, keepdims=True)
            y_ref[rows, cols] = (yz * lax.rsqrt(ms + NORM_EPS) * ssd_nw_ref[:, cols]).astype(BF16)

        for h in range(ML_HEADS):
            cols = slice(h * ML_HEAD_DIM, (h + 1) * ML_HEAD_DIM)
            b_col = tab[:, F_COL + h:F_COL + h + 1]
            li_col = tab[:, I_COL + h:I_COL + h + 1]
            b_row = tab_t[F_COL + h:F_COL + h + 1, :]
            li_row = tab_t[I_COL + h:I_COL + h + 1, :]
            b_end = tab_end[:, F_COL + h:F_COL + h + 1]
            m_prev = m_ref[0:1, h:h + 1]
            d_log = jnp.where(causal, b_col - b_row + li_row, neg_inf)
            inter_log = b_col + m_prev
            m_t = jnp.maximum(inter_log, jnp.max(d_log, axis=1, keepdims=True))
            w_intra = jnp.exp(d_log - m_t)
            w_inter = jnp.exp(inter_log - m_t)
            q_h = q_c[rows, cols] * jnp.asarray(ML_HEAD_DIM ** -0.5, BF16)
            k_h = k_c[rows, cols]
            v_h = v_ref[rows, cols]
            c_prev = c_ref[h]
            n_prev = n_ref[h]
            qk = _dot_nt(q_h, k_h) * w_intra
            num = _dot(qk.astype(BF16), v_h) + w_inter * _dot(q_h, c_prev.astype(BF16))
            den = (jnp.sum(qk, axis=1, keepdims=True)
                   + w_inter * jnp.sum(q_h.astype(F32) * n_prev, axis=1, keepdims=True))
            hh = num / jnp.maximum(jnp.abs(den), jnp.exp(-m_t))
            ms = jnp.mean(hh * hh, axis=-1, keepdims=True)
            hn = hh * lax.rsqrt(ms + NORM_EPS) * ml_nw_ref[:, cols]
            hm_ref[rows, cols] = (_sigmoid(o_ref[rows, cols].astype(F32)) * hn).astype(BF16)
            g_col = b_end - b_col + li_col
            m_loc = jnp.max(g_col, axis=0, keepdims=True)
            wg = jnp.exp(g_col - m_loc)
            kw = k_h.astype(F32) * wg
            c_loc = _dot_tn(kw.astype(BF16), v_h)
            n_loc = jnp.sum(kw, axis=0, keepdims=True)
            m_new = jnp.maximum(b_end + m_prev, m_loc)
            a_old = jnp.exp(b_end + m_prev - m_new)
            a_loc = jnp.exp(m_loc - m_new)
            c_ref[h] = a_old * c_prev + a_loc * c_loc
            n_ref[h] = a_old * n_prev + a_loc * n_loc
            m_ref[0:1, h:h + 1] = m_new
        return carry

    lax.fori_loop(0, n_chunks, chunk_body, 0)


def _mixers(main, small, cw_xs, cb_xs, cw_bc, cb_bc, cw_q, cb_q, cw_k, cb_k,
            bias_row, alog_row, dskip_row, ssd_nw, ml_nw, batch, seq):
    tb, L = MIX_TOKENS, SCAN_CHUNK
    spb = seq // tb
    tokens = batch * seq

    def col(width, idx):
        return pl.BlockSpec((tb, width), lambda b, s: (b * spb + s, idx))

    def const(shape):
        return pl.BlockSpec(shape, lambda b, s: tuple(0 for _ in shape))

    in_specs = [
        col(2048, 0),
        col(2048, 1),
        col(1024, 4),
        col(1024, 5),
        col(1024, 6),
        col(1024, 7),
        col(1024, 8),
        pl.BlockSpec((tb, SMALL_WIDTH), lambda b, s: (b * spb + s, 0)),
        const((CONV_TAPS, 2048)), const((1, 2048)),
        const((CONV_TAPS, 1024)), const((1, 1024)),
        const((CONV_TAPS, 1024)), const((1, 1024)),
        const((CONV_TAPS, 1024)), const((1, 1024)),
        const((1, SMALL_WIDTH)), const((1, SMALL_WIDTH)),
        const((1, SSD_INNER)), const((1, SSD_INNER)), const((1, ML_INNER)),
    ]
    out_specs = [
        pl.BlockSpec((tb, SSD_INNER), lambda b, s: (b * spb + s, 0)),
        pl.BlockSpec((tb, ML_INNER), lambda b, s: (b * spb + s, 0)),
    ]
    scratch = [
        pltpu.VMEM((tb + SUBLANES, 2048), F32), pltpu.VMEM((tb + SUBLANES, 1024), F32),
        pltpu.VMEM((tb + SUBLANES, 1024), F32), pltpu.VMEM((tb + SUBLANES, 1024), F32),
        pltpu.VMEM((tb, 2048), BF16), pltpu.VMEM((tb, 1024), BF16),
        pltpu.VMEM((tb, 1024), BF16), pltpu.VMEM((tb, 1024), BF16),
        pltpu.VMEM((SSD_STATE, SSD_INNER), F32),
        pltpu.VMEM((ML_HEADS, ML_HEAD_DIM, ML_HEAD_DIM), F32),
        pltpu.VMEM((ML_HEADS, 1, ML_HEAD_DIM), F32),
        pltpu.VMEM((SUBLANES, LANES), F32),
        pltpu.VMEM((tb, SMALL_WIDTH), F32),
        pltpu.VMEM((tb // L, SMALL_WIDTH, L), F32),
        pltpu.VMEM((tb, SMALL_WIDTH), F32),
        pltpu.VMEM((L, SSD_INNER), F32),
    ]
    return pl.pallas_call(
        _mixer_kernel,
        grid=(batch, spb),
        in_specs=in_specs,
        out_specs=out_specs,
        out_shape=[jax.ShapeDtypeStruct((tokens, SSD_INNER), BF16),
                   jax.ShapeDtypeStruct((tokens, ML_INNER), BF16)],
        scratch_shapes=scratch,
        compiler_params=pltpu.CompilerParams(
            dimension_semantics=("arbitrary", "arbitrary"),
            vmem_limit_bytes=VMEM_LIMIT),
        name="mixers",
    )(main, main, main, main, main, main, main, small,
      cw_xs, cb_xs, cw_bc, cb_bc, cw_q, cb_q, cw_k, cb_k,
      bias_row, alog_row, dskip_row, ssd_nw, ml_nw)


def _rms(x, w):
    ms = jnp.mean(x * x, axis=-1, keepdims=True)
    return x * lax.rsqrt(ms + NORM_EPS) * w


def _merge_kernel(x_ref, y_ref, hm_ref, gs_ref, gm_ref,
                  wbs_ref, wbm_ref, wo_ref, nmw_ref, wup_ref, wdn_ref, nfw_ref, out_ref):
    a = _dot(y_ref[...], wbs_ref[...])
    b = _dot(hm_ref[...], wbm_ref[...])
    mixed = _sigmoid(gs_ref[...].astype(F32)) * a + _sigmoid(gm_ref[...].astype(F32)) * b
    h1 = x_ref[...] + _dot(mixed.astype(BF16), wo_ref[...])
    u = _rms(h1, nmw_ref[...]).astype(BF16)
    up = jnp.maximum(_dot(u, wup_ref[...]), 0.0)
    act = (up * up).astype(BF16)
    h2 = h1 + _dot(act, wdn_ref[...])
    out_ref[...] = _rms(h2, nfw_ref[...])


def _merge(x2, y, hm, main, w_br_ssd, w_br_ml, w_out, norm_mlp_w, w_up, w_down, norm_final_w):
    tokens = x2.shape[0]
    tm = MERGE_TM

    def resident(shape):
        return pl.BlockSpec(shape, lambda i: (0, 0), pipeline_mode=pl.Buffered(1))

    return pl.pallas_call(
        _merge_kernel,
        grid=(tokens // tm,),
        in_specs=[
            pl.BlockSpec((tm, D_MODEL), lambda i: (i, 0)),
            pl.BlockSpec((tm, SSD_INNER), lambda i: (i, 0)),
            pl.BlockSpec((tm, ML_INNER), lambda i: (i, 0)),
            pl.BlockSpec((tm, COL_BLOCK), lambda i: (i, 9)),
            pl.BlockSpec((tm, COL_BLOCK), lambda i: (i, 10)),
            resident((SSD_INNER, D_MODEL)),
            resident((ML_INNER, D_MODEL)),
            resident((D_MODEL, D_MODEL)),
            resident((1, D_MODEL)),
            resident((D_MODEL, D_FF)),
            resident((D_FF, D_MODEL)),
            resident((1, D_MODEL)),
        ],
        out_specs=pl.BlockSpec((tm, D_MODEL), lambda i: (i, 0)),
        out_shape=jax.ShapeDtypeStruct((tokens, D_MODEL), F32),
        compiler_params=pltpu.CompilerParams(
            dimension_semantics=("arbitrary",),
            vmem_limit_bytes=VMEM_LIMIT),
        name="merge_mlp",
    )(x2, y, hm, main, main, w_br_ssd, w_br_ml, w_out, norm_mlp_w, w_up, w_down, norm_final_w)


def _pad_row(parts, width):
    row = jnp.zeros((width,), F32)
    for off, val in parts:
        row = lax.dynamic_update_slice(row, val.astype(F32), (off,))
    return row.reshape(1, width)


def kernel(x, norm_mix_w, w_in, conv_ssd_w, conv_ssd_b, dt_bias, a_log, d_skip, ssd_norm_w,
           conv_qk_w, conv_qk_b, i_bias, f_bias, mlstm_norm_w, w_br_ssd, w_br_mlstm, w_out,
           norm_mlp_w, w_up, w_down, norm_final_w):
    batch, seq, _ = x.shape
    x2 = x.reshape(batch * seq, D_MODEL)
    layer = 0

    w = w_in[layer]
    o_z, o_xbc, o_dt = 0, SSD_INNER, SSD_INNER + 3072
    o_q = o_dt + SSD_HEADS
    o_k, o_v, o_o = o_q + ML_INNER, o_q + 2 * ML_INNER, o_q + 3 * ML_INNER
    o_i = o_q + 4 * ML_INNER
    o_f = o_i + ML_HEADS
    o_g = o_f + ML_HEADS
    w_main = jnp.concatenate(
        [w[:, o_z:o_xbc], w[:, o_xbc:o_dt], w[:, o_q:o_i], w[:, o_g:o_g + 2 * D_MODEL]], axis=1).astype(BF16)
    w_small = jnp.concatenate(
        [w[:, o_dt:o_q], w[:, o_i:o_g],
         jnp.zeros((D_MODEL, SMALL_WIDTH - SSD_HEADS - 2 * ML_HEADS), F32)], axis=1).astype(BF16)

    main, small = _in_proj(x2, norm_mix_w[layer].reshape(1, D_MODEL), w_main, w_small)

    cw, cb = conv_ssd_w[layer], conv_ssd_b[layer]
    cqk, cbqk = conv_qk_w[layer], conv_qk_b[layer]
    bias_row = _pad_row([(DT_COL, dt_bias[layer]), (I_COL, i_bias[layer]), (F_COL, f_bias[layer])], SMALL_WIDTH)
    alog_row = _pad_row([(DT_COL, a_log[layer])], SMALL_WIDTH)
    dskip_row = jnp.repeat(d_skip[layer].astype(F32), SSD_HEAD_DIM).reshape(1, SSD_INNER)

    y, hm = _mixers(
        main, small,
        cw[:, :SSD_INNER], cb[:SSD_INNER].reshape(1, -1),
        cw[:, SSD_INNER:], cb[SSD_INNER:].reshape(1, -1),
        cqk[:, :ML_INNER], cbqk[:ML_INNER].reshape(1, -1),
        cqk[:, ML_INNER:], cbqk[ML_INNER:].reshape(1, -1),
        bias_row, alog_row, dskip_row,
        ssd_norm_w[layer].reshape(1, SSD_INNER), mlstm_norm_w[layer].reshape(1, ML_INNER),
        batch, seq)

    out = _merge(x2, y, hm, main,
                 w_br_ssd[layer].astype(BF16), w_br_mlstm[layer].astype(BF16), w_out[layer].astype(BF16),
                 norm_mlp_w[layer].reshape(1, D_MODEL), w_up[layer].astype(BF16), w_down[layer].astype(BF16),
                 norm_final_w.reshape(1, D_MODEL))
    return out.reshape(batch, seq, D_MODEL)
```

```python
import jax
import jax.numpy as jnp
import numpy as np
from jax import lax
from jax.experimental import pallas as pl
from jax.experimental.pallas import tpu as pltpu

F32 = jnp.float32
BF16 = jnp.bfloat16

D_MODEL = 1024
NORM_EPS = 1e-5
CONV_TAPS = 4
SSD_INNER = 2048
SSD_HEAD_DIM = 64
SSD_HEADS = 32
SSD_GROUPS = 4
SSD_STATE = 128
SSD_GROUP_WIDTH = SSD_INNER // SSD_GROUPS
ML_INNER = 1024
ML_HEADS = 4
ML_HEAD_DIM = 256
D_FF = 4096

LANES = 128
BF16_ROWS = 16

COL_BLOCK = 1024
MAIN_WIDTH = 11 * COL_BLOCK
SMALL_WIDTH = LANES
DT_COL, I_COL, F_COL = 0, 32, 36
DT_ROW_COPY = 64

SCAN_CHUNK = 128
MIX_TOKENS = 256
CONV_PIECES = (112, 112, 32)
CONV_STRIP = 256
PROJ_TM = 1024
MERGE_TM = 512
HEADS_PER_DOT = 4
VMEM_LIMIT = 56 * 1024 * 1024


def _sigmoid(x):
    return 0.5 * jnp.tanh(0.5 * x) + 0.5


def _silu(x):
    h = 0.5 * x
    return h * jnp.tanh(h) + h


def _softplus(x):
    return jnp.maximum(x, 0.0) + jnp.log1p(jnp.exp(-jnp.abs(x)))


def _dot(a, b):
    return jnp.dot(a, b, preferred_element_type=F32)


def _dot_nt(a, b):
    return lax.dot_general(a, b, (((1,), (1,)), ((), ())), preferred_element_type=F32)


def _dot_tn(a, b):
    return lax.dot_general(a, b, (((0,), (0,)), ((), ())), preferred_element_type=F32)


def _in_proj_kernel(x_ref, nw_ref, w_ref, ws_ref, main_ref, small_ref, u_ref):
    j = pl.program_id(1)

    @pl.when(j == 0)
    def _():
        x = x_ref[...]
        ms = jnp.mean(x * x, axis=-1, keepdims=True)
        u = (x * lax.rsqrt(ms + NORM_EPS) * nw_ref[...]).astype(BF16)
        u_ref[...] = u
        small_ref[...] = _dot(u, ws_ref[...])

    main_ref[...] = _dot(u_ref[...], w_ref[...]).astype(BF16)


def _in_proj(x2, norm_w, w_main, w_small):
    tokens = x2.shape[0]
    grid = (tokens // PROJ_TM, MAIN_WIDTH // COL_BLOCK)
    return pl.pallas_call(
        _in_proj_kernel,
        grid=grid,
        in_specs=[
            pl.BlockSpec((PROJ_TM, D_MODEL), lambda i, j: (i, 0)),
            pl.BlockSpec((1, D_MODEL), lambda i, j: (0, 0)),
            pl.BlockSpec((D_MODEL, COL_BLOCK), lambda i, j: (0, j)),
            pl.BlockSpec((D_MODEL, SMALL_WIDTH), lambda i, j: (0, 0)),
        ],
        out_specs=[
            pl.BlockSpec((PROJ_TM, COL_BLOCK), lambda i, j: (i, j)),
            pl.BlockSpec((PROJ_TM, SMALL_WIDTH), lambda i, j: (i, 0)),
        ],
        out_shape=[
            jax.ShapeDtypeStruct((tokens, MAIN_WIDTH), BF16),
            jax.ShapeDtypeStruct((tokens, SMALL_WIDTH), F32),
        ],
        scratch_shapes=[pltpu.VMEM((PROJ_TM, D_MODEL), BF16)],
        compiler_params=pltpu.CompilerParams(
            dimension_semantics=("arbitrary", "arbitrary"),
            vmem_limit_bytes=VMEM_LIMIT),
        name="in_proj",
    )(x2, norm_w, w_main, w_small)


def _shift_select_matrix(p):
    h = BF16_ROWS
    sel = np.zeros((p, CONV_TAPS * (p + h)), np.float32)
    for tap in range(CONV_TAPS):
        for t in range(p):
            sel[t, tap * (p + h) + t + h - (CONV_TAPS - 1 - tap)] = 1.0
    return jnp.asarray(sel, BF16)


def _head_expand_matrix():
    e = np.zeros((2 * LANES, SSD_INNER), np.float32)
    for h in range(SSD_HEADS):
        e[h, h * SSD_HEAD_DIM:(h + 1) * SSD_HEAD_DIM] = 1.0
        e[LANES + h, h * SSD_HEAD_DIM:(h + 1) * SSD_HEAD_DIM] = 1.0
    return jnp.asarray(e, BF16)


def _conv_silu(src_ref, hist_ref, sels, w_ref, b_ref, dst_ref, width):
    tb, h = MIX_TOKENS, BF16_ROWS
    for c0 in range(0, width, CONV_STRIP):
        cols = slice(c0, c0 + CONV_STRIP)
        taps = [w_ref[tap:tap + 1, cols].astype(BF16) for tap in range(CONV_TAPS)]
        r0 = 0
        for p in CONV_PIECES:
            if r0 == 0:
                win = jnp.concatenate([hist_ref[:, cols], src_ref[0:p, cols]], axis=0)
            else:
                win = src_ref[r0 - h:r0 + p, cols]
            scaled = jnp.concatenate([win * taps[tap] for tap in range(CONV_TAPS)], axis=0)
            acc = _dot(sels[p], scaled) + b_ref[:, cols]
            dst_ref[r0:r0 + p, cols] = _silu(acc).astype(BF16)
            r0 += p
        hist_ref[:, cols] = src_ref[tb - h:tb, cols]


def _mixer_kernel(z_ref, xs_ref, bc_ref, q_ref, k_ref, v_ref, o_ref, sm_ref,
                  cw_xs_ref, cb_xs_ref, cw_bc_ref, cb_bc_ref, cw_q_ref, cb_q_ref, cw_k_ref, cb_k_ref,
                  bias_ref, alog_ref, dskip_ref, ssd_nw_ref, ml_nw_ref, sel_a_ref, sel_b_ref, e2_ref, eye_ref,
                  y_ref, hm_ref,
                  hist_xs, hist_bc, hist_q, hist_k, xs_c, bc_c, q_c, k_c,
                  s_ref, c_ref, n_ref, m_ref, p_ref, pt_ref, dt_ref):
    tb, L = MIX_TOKENS, SCAN_CHUNK
    n_chunks = tb // L
    hp = HEADS_PER_DOT
    qw = hp * SSD_HEAD_DIM
    gw = SSD_GROUP_WIDTH

    @pl.when(pl.program_id(1) == 0)
    def _():
        for hist in (hist_xs, hist_bc, hist_q, hist_k):
            hist[...] = jnp.zeros_like(hist)
        s_ref[...] = jnp.zeros_like(s_ref)
        c_ref[...] = jnp.zeros_like(c_ref)
        n_ref[...] = jnp.zeros_like(n_ref)
        m_ref[...] = jnp.zeros_like(m_ref)

    sels = {CONV_PIECES[0]: sel_a_ref[...], CONV_PIECES[-1]: sel_b_ref[...]}
    _conv_silu(xs_ref, hist_xs, sels, cw_xs_ref, cb_xs_ref, xs_c, SSD_INNER)
    _conv_silu(bc_ref, hist_bc, sels, cw_bc_ref, cb_bc_ref, bc_c, 2 * SSD_GROUPS * SSD_STATE)
    _conv_silu(q_ref, hist_q, sels, cw_q_ref, cb_q_ref, q_c, ML_INNER)
    _conv_silu(k_ref, hist_k, sels, cw_k_ref, cb_k_ref, k_c, ML_INNER)

    sm = sm_ref[...] + bias_ref[...]
    lane = lax.broadcasted_iota(jnp.int32, (tb, SMALL_WIDTH), 1)
    dt = _softplus(sm)
    log_f = -_softplus(-sm)
    a_row = -jnp.exp(alog_ref[...])
    is_dt = lane < I_COL
    is_i = (lane >= I_COL) & (lane < F_COL)
    is_f = (lane >= F_COL) & (lane < F_COL + ML_HEADS)
    pre = jnp.where(is_dt, dt * a_row, jnp.where(is_f, log_f, 0.0))
    rt = lax.broadcasted_iota(jnp.int32, (tb, tb), 0)
    rs = lax.broadcasted_iota(jnp.int32, (tb, tb), 1)
    tri = ((rs <= rt) & ((rt // L) == (rs // L))).astype(F32)
    cs = jnp.dot(tri, pre, preferred_element_type=F32, precision=lax.Precision.HIGHEST)
    dt_copy = pltpu.roll(dt, DT_ROW_COPY, axis=1)
    is_dt_copy = (lane >= DT_ROW_COPY) & (lane < DT_ROW_COPY + SSD_HEADS)
    table = jnp.where(is_dt | is_f, cs, jnp.where(is_i, sm, jnp.where(is_dt_copy, dt_copy, 0.0)))
    p_ref[...] = table
    dt_ref[...] = jnp.where(is_dt, dt, 0.0)
    for c in range(n_chunks):
        pt_ref[c] = table[c * L:(c + 1) * L, :].T

    causal = (lax.broadcasted_iota(jnp.int32, (L, L), 0) >= lax.broadcasted_iota(jnp.int32, (L, L), 1))
    dt_lanes = lax.broadcasted_iota(jnp.int32, (L, SMALL_WIDTH), 1) < I_COL
    lane_q = lax.broadcasted_iota(jnp.int32, (1, qw), 1)
    head_masks = [(lane_q >= a * SSD_HEAD_DIM) & (lane_q < (a + 1) * SSD_HEAD_DIM) for a in range(hp)]
    neg_inf = jnp.float32(-jnp.inf)

    def split_hi_lo(v):
        hi = v.astype(BF16)
        lo = (v - hi.astype(F32)).astype(BF16)
        return jnp.concatenate([hi, lo], axis=1)

    def chunk_body(c, carry):
        r0 = c * L
        rows = slice(r0, r0 + L)
        tab = p_ref[rows, :]
        tab_end = p_ref[r0 + L - 1:r0 + L, :]
        tab_t = pt_ref[c]
        dt_blk = dt_ref[rows, :]
        ea2 = split_hi_lo(jnp.where(dt_lanes, jnp.exp(tab), 0.0))
        td2 = split_hi_lo(jnp.where(dt_lanes, jnp.exp(tab_end - tab) * dt_blk, 0.0))

        for g in range(SSD_GROUPS):
            gcols = slice(g * gw, (g + 1) * gw)
            b_mat = bc_c[rows, g * SSD_STATE:(g + 1) * SSD_STATE]
            c_mat = bc_c[rows, (SSD_GROUPS + g) * SSD_STATE:(SSD_GROUPS + g + 1) * SSD_STATE]
            cb = _dot_nt(c_mat, b_mat)
            ea_g = _dot(ea2, e2_ref[:, gcols])
            td_g = _dot(td2, e2_ref[:, gcols])
            x_g = xs_c[rows, gcols]
            x_gf = x_g.astype(F32)
            s_g = s_ref[:, gcols]
            inter = _dot(c_mat, s_g.astype(BF16)) * ea_g
            intra_parts = []
            for qd in range(gw // qw):
                x_q = x_g[:, qd * qw:(qd + 1) * qw]
                zero_x = jnp.zeros_like(x_q)
                lhs_parts, rhs_parts = [], []
                for a in range(hp):
                    h = (g * gw + qd * qw) // SSD_HEAD_DIM + a
                    seg = tab[:, h:h + 1] - tab_t[h:h + 1, :]
                    dec = jnp.exp(jnp.where(causal, seg, neg_inf))
                    w_mat = cb * dec * tab_t[DT_ROW_COPY + h:DT_ROW_COPY + h + 1, :]
                    lhs_parts.append(w_mat.astype(BF16))
                    rhs_parts.append(jnp.where(head_masks[a], x_q, zero_x))
                intra_parts.append(_dot(jnp.concatenate(lhs_parts, axis=1),
                                        jnp.concatenate(rhs_parts, axis=0)))
            y_g = jnp.concatenate(intra_parts, axis=1) + inter + dskip_ref[:, gcols] * x_gf
            yz = y_g * _silu(z_ref[rows, gcols].astype(F32))
            ms = jnp.mean(yz * yz, axis=-1, keepdims=True)
            y_ref[rows, gcols] = (yz * lax.rsqrt(ms + NORM_EPS) * ssd_nw_ref[:, gcols]).astype(BF16)
            xw = (x_gf * td_g).astype(BF16)
            b_t = _dot_nt(eye_ref[0:SSD_STATE, 0:SSD_STATE], b_mat).astype(BF16)
            s_ref[:, gcols] = s_g * ea_g[L - 1:L, :] + _dot(b_t, xw)

        heads = range(ML_HEADS)
        hcols = [slice(h * ML_HEAD_DIM, (h + 1) * ML_HEAD_DIM) for h in heads]
        q_hs = [q_c[rows, hcols[h]] * jnp.asarray(ML_HEAD_DIM ** -0.5, BF16) for h in heads]
        k_hs = [k_c[rows, hcols[h]] for h in heads]
        v_hs = [v_ref[rows, hcols[h]] for h in heads]
        c_prevs = [c_ref[h] for h in heads]
        n_prevs = [n_ref[h] for h in heads]
        qk_raw = [_dot_nt(q_hs[h], k_hs[h]) for h in heads]
        q_c_prev = [_dot(q_hs[h], c_prevs[h].astype(BF16)) for h in heads]
        k_ts = [_dot_nt(eye_ref[...], k_hs[h]) for h in heads]
        w_intra, w_inter, m_ts, wg2s, kw_ts, m_news, a_olds, a_locs = [], [], [], [], [], [], [], []
        for h in heads:
            b_col = tab[:, F_COL + h:F_COL + h + 1]
            b_row = tab_t[F_COL + h:F_COL + h + 1, :]
            li_row = tab_t[I_COL + h:I_COL + h + 1, :]
            b_end = tab_end[:, F_COL + h:F_COL + h + 1]
            m_prev = m_ref[0:1, h:h + 1]
            d_log = jnp.where(causal, b_col - b_row + li_row, neg_inf)
            inter_log = b_col + m_prev
            m_t = jnp.maximum(inter_log, jnp.max(d_log, axis=1, keepdims=True))
            w_intra.append(jnp.exp(d_log - m_t))
            w_inter.append(jnp.exp(inter_log - m_t))
            m_ts.append(m_t)
            g_row = b_end - b_row + li_row
            m_loc = jnp.max(g_row, axis=1, keepdims=True)
            wg_row = jnp.exp(g_row - m_loc)
            kw_ts.append((k_ts[h] * wg_row).astype(BF16))
            wg_hi = wg_row.astype(BF16)
            wg_lo = (wg_row - wg_hi.astype(F32)).astype(BF16)
            wg2s.append(jnp.concatenate([wg_hi, wg_lo], axis=1))
            m_new = jnp.maximum(b_end + m_prev, m_loc)
            m_news.append(m_new)
            a_olds.append(jnp.exp(b_end + m_prev - m_new))
            a_locs.append(jnp.exp(m_loc - m_new))
        c_locs = [_dot(kw_ts[h], v_hs[h]) for h in heads]
        n_locs = [_dot(wg2s[h], jnp.concatenate([k_hs[h], k_hs[h]], axis=0)) for h in heads]
        qks = [qk_raw[h] * w_intra[h] for h in heads]
        nums = [_dot(qks[h].astype(BF16), v_hs[h]) + w_inter[h] * q_c_prev[h] for h in heads]
        for h in heads:
            den = (jnp.sum(qks[h], axis=1, keepdims=True)
                   + w_inter[h] * jnp.sum(q_hs[h].astype(F32) * n_prevs[h], axis=1, keepdims=True))
            hh = nums[h] / jnp.maximum(jnp.abs(den), jnp.exp(-m_ts[h]))
            ms = jnp.mean(hh * hh, axis=-1, keepdims=True)
            hn = hh * lax.rsqrt(ms + NORM_EPS) * ml_nw_ref[:, hcols[h]]
            hm_ref[rows, hcols[h]] = (_sigmoid(o_ref[rows, hcols[h]].astype(F32)) * hn).astype(BF16)
        for h in heads:
            c_ref[h] = a_olds[h] * c_prevs[h] + a_locs[h] * c_locs[h]
            n_ref[h] = a_olds[h] * n_prevs[h] + a_locs[h] * n_locs[h]
            m_ref[0:1, h:h + 1] = m_news[h]
        return carry

    for c in range(n_chunks):
        chunk_body(c, 0)


def _mixers(main, small, cw_xs, cb_xs, cw_bc, cb_bc, cw_q, cb_q, cw_k, cb_k,
            bias_row, alog_row, dskip_row, ssd_nw, ml_nw, batch, seq):
    tb, L = MIX_TOKENS, SCAN_CHUNK
    spb = seq // tb
    tokens = batch * seq
    assert sum(CONV_PIECES) == tb and len(set(CONV_PIECES)) == 2
    sel_a = _shift_select_matrix(CONV_PIECES[0])
    sel_b = _shift_select_matrix(CONV_PIECES[-1])
    e2 = _head_expand_matrix()
    eye = jnp.eye(ML_HEAD_DIM, dtype=BF16)

    def col(width, idx):
        return pl.BlockSpec((tb, width), lambda b, s: (b * spb + s, idx))

    def const(shape):
        return pl.BlockSpec(shape, lambda b, s: tuple(0 for _ in shape))

    in_specs = [
        col(2048, 0),
        col(2048, 1),
        col(1024, 4),
        col(1024, 5),
        col(1024, 6),
        col(1024, 7),
        col(1024, 8),
        pl.BlockSpec((tb, SMALL_WIDTH), lambda b, s: (b * spb + s, 0)),
        const((CONV_TAPS, 2048)), const((1, 2048)),
        const((CONV_TAPS, 1024)), const((1, 1024)),
        const((CONV_TAPS, 1024)), const((1, 1024)),
        const((CONV_TAPS, 1024)), const((1, 1024)),
        const((1, SMALL_WIDTH)), const((1, SMALL_WIDTH)),
        const((1, SSD_INNER)), const((1, SSD_INNER)), const((1, ML_INNER)),
        const(sel_a.shape), const(sel_b.shape), const(e2.shape), const(eye.shape),
    ]
    out_specs = [
        pl.BlockSpec((tb, SSD_INNER), lambda b, s: (b * spb + s, 0)),
        pl.BlockSpec((tb, ML_INNER), lambda b, s: (b * spb + s, 0)),
    ]
    scratch = [
        pltpu.VMEM((BF16_ROWS, 2048), BF16), pltpu.VMEM((BF16_ROWS, 1024), BF16),
        pltpu.VMEM((BF16_ROWS, 1024), BF16), pltpu.VMEM((BF16_ROWS, 1024), BF16),
        pltpu.VMEM((tb, 2048), BF16), pltpu.VMEM((tb, 1024), BF16),
        pltpu.VMEM((tb, 1024), BF16), pltpu.VMEM((tb, 1024), BF16),
        pltpu.VMEM((SSD_STATE, SSD_INNER), F32),
        pltpu.VMEM((ML_HEADS, ML_HEAD_DIM, ML_HEAD_DIM), F32),
        pltpu.VMEM((ML_HEADS, 1, ML_HEAD_DIM), F32),
        pltpu.VMEM((8, LANES), F32),
        pltpu.VMEM((tb, SMALL_WIDTH), F32),
        pltpu.VMEM((tb // L, SMALL_WIDTH, L), F32),
        pltpu.VMEM((tb, SMALL_WIDTH), F32),
    ]
    return pl.pallas_call(
        _mixer_kernel,
        grid=(batch, spb),
        in_specs=in_specs,
        out_specs=out_specs,
        out_shape=[jax.ShapeDtypeStruct((tokens, SSD_INNER), BF16),
                   jax.ShapeDtypeStruct((tokens, ML_INNER), BF16)],
        scratch_shapes=scratch,
        compiler_params=pltpu.CompilerParams(
            dimension_semantics=("arbitrary", "arbitrary"),
            vmem_limit_bytes=VMEM_LIMIT),
        name="mixers",
    )(main, main, main, main, main, main, main, small,
      cw_xs, cb_xs, cw_bc, cb_bc, cw_q, cb_q, cw_k, cb_k,
      bias_row, alog_row, dskip_row, ssd_nw, ml_nw, sel_a, sel_b, e2, eye)


def _rms(x, w):
    ms = jnp.mean(x * x, axis=-1, keepdims=True)
    return x * lax.rsqrt(ms + NORM_EPS) * w


def _merge_kernel(x_ref, y_ref, hm_ref, gs_ref, gm_ref,
                  wbs_ref, wbm_ref, wo_ref, nmw_ref, wup_ref, wdn_ref, nfw_ref, out_ref):
    a = _dot(y_ref[...], wbs_ref[...])
    b = _dot(hm_ref[...], wbm_ref[...])
    mixed = _sigmoid(gs_ref[...].astype(F32)) * a + _sigmoid(gm_ref[...].astype(F32)) * b
    h1 = x_ref[...] + _dot(mixed.astype(BF16), wo_ref[...])
    u = _rms(h1, nmw_ref[...]).astype(BF16)
    up = jnp.maximum(_dot(u, wup_ref[...]), 0.0)
    act = (up * up).astype(BF16)
    h2 = h1 + _dot(act, wdn_ref[...])
    out_ref[...] = _rms(h2, nfw_ref[...])


def _merge(x2, y, hm, main, w_br_ssd, w_br_ml, w_out, norm_mlp_w, w_up, w_down, norm_final_w):
    tokens = x2.shape[0]
    tm = MERGE_TM

    def resident(shape):
        return pl.BlockSpec(shape, lambda i: (0, 0), pipeline_mode=pl.Buffered(1))

    return pl.pallas_call(
        _merge_kernel,
        grid=(tokens // tm,),
        in_specs=[
            pl.BlockSpec((tm, D_MODEL), lambda i: (i, 0)),
            pl.BlockSpec((tm, SSD_INNER), lambda i: (i, 0)),
            pl.BlockSpec((tm, ML_INNER), lambda i: (i, 0)),
            pl.BlockSpec((tm, COL_BLOCK), lambda i: (i, 9)),
            pl.BlockSpec((tm, COL_BLOCK), lambda i: (i, 10)),
            resident((SSD_INNER, D_MODEL)),
            resident((ML_INNER, D_MODEL)),
            resident((D_MODEL, D_MODEL)),
            resident((1, D_MODEL)),
            resident((D_MODEL, D_FF)),
            resident((D_FF, D_MODEL)),
            resident((1, D_MODEL)),
        ],
        out_specs=pl.BlockSpec((tm, D_MODEL), lambda i: (i, 0)),
        out_shape=jax.ShapeDtypeStruct((tokens, D_MODEL), F32),
        compiler_params=pltpu.CompilerParams(
            dimension_semantics=("arbitrary",),
            vmem_limit_bytes=VMEM_LIMIT),
        name="merge_mlp",
    )(x2, y, hm, main, main, w_br_ssd, w_br_ml, w_out, norm_mlp_w, w_up, w_down, norm_final_w)


def _pad_row(parts, width):
    row = jnp.zeros((width,), F32)
    for off, val in parts:
        row = lax.dynamic_update_slice(row, val.astype(F32), (off,))
    return row.reshape(1, width)


def kernel(x, norm_mix_w, w_in, conv_ssd_w, conv_ssd_b, dt_bias, a_log, d_skip, ssd_norm_w,
           conv_qk_w, conv_qk_b, i_bias, f_bias, mlstm_norm_w, w_br_ssd, w_br_mlstm, w_out,
           norm_mlp_w, w_up, w_down, norm_final_w):
    batch, seq, _ = x.shape
    x2 = x.reshape(batch * seq, D_MODEL)
    layer = 0

    w = w_in[layer]
    o_z, o_xbc, o_dt = 0, SSD_INNER, SSD_INNER + 3072
    o_q = o_dt + SSD_HEADS
    o_i = o_q + 4 * ML_INNER
    o_f = o_i + ML_HEADS
    o_g = o_f + ML_HEADS
    w_main = jnp.concatenate(
        [w[:, o_z:o_xbc], w[:, o_xbc:o_dt], w[:, o_q:o_i], w[:, o_g:o_g + 2 * D_MODEL]], axis=1).astype(BF16)
    w_small = jnp.concatenate(
        [w[:, o_dt:o_q], w[:, o_i:o_g],
         jnp.zeros((D_MODEL, SMALL_WIDTH - SSD_HEADS - 2 * ML_HEADS), F32)], axis=1).astype(BF16)

    main, small = _in_proj(x2, norm_mix_w[layer].reshape(1, D_MODEL), w_main, w_small)

    cw, cb = conv_ssd_w[layer], conv_ssd_b[layer]
    cqk, cbqk = conv_qk_w[layer], conv_qk_b[layer]
    bias_row = _pad_row([(DT_COL, dt_bias[layer]), (I_COL, i_bias[layer]), (F_COL, f_bias[layer])], SMALL_WIDTH)
    alog_row = _pad_row([(DT_COL, a_log[layer])], SMALL_WIDTH)
    dskip_row = jnp.repeat(d_skip[layer].astype(F32), SSD_HEAD_DIM).reshape(1, SSD_INNER)

    y, hm = _mixers(
        main, small,
        cw[:, :SSD_INNER], cb[:SSD_INNER].reshape(1, -1),
        cw[:, SSD_INNER:], cb[SSD_INNER:].reshape(1, -1),
        cqk[:, :ML_INNER], cbqk[:ML_INNER].reshape(1, -1),
        cqk[:, ML_INNER:], cbqk[ML_INNER:].reshape(1, -1),
        bias_row, alog_row, dskip_row,
        ssd_norm_w[layer].reshape(1, SSD_INNER), mlstm_norm_w[layer].reshape(1, ML_INNER),
        batch, seq)

    out = _merge(x2, y, hm, main,
                 w_br_ssd[layer].astype(BF16), w_br_mlstm[layer].astype(BF16), w_out[layer].astype(BF16),
                 norm_mlp_w[layer].reshape(1, D_MODEL), w_up[layer].astype(BF16), w_down[layer].astype(BF16),
                 norm_final_w.reshape(1, D_MODEL))
    return out.reshape(batch, seq, D_MODEL)
```

```python
import jax
import jax.numpy as jnp
import numpy as np
from jax import lax
from jax.experimental import pallas as pl
from jax.experimental.pallas import tpu as pltpu

F32 = jnp.float32
BF16 = jnp.bfloat16

D_MODEL = 1024
NORM_EPS = 1e-5
CONV_TAPS = 4
SSD_INNER = 2048
SSD_HEAD_DIM = 64
SSD_HEADS = 32
SSD_GROUPS = 4
SSD_STATE = 128
SSD_GROUP_WIDTH = SSD_INNER // SSD_GROUPS
ML_INNER = 1024
ML_HEADS = 4
ML_HEAD_DIM = 256
D_FF = 4096

LANES = 128
SUBLANES = 8

COL_BLOCK = 1024
N_COL_BLOCKS = 11
MAIN_WIDTH = N_COL_BLOCKS * COL_BLOCK
CONV_FIRST_BLOCK, CONV_LAST_BLOCK = 2, 6
CONV_WIDTH = (CONV_LAST_BLOCK - CONV_FIRST_BLOCK + 1) * COL_BLOCK
SMALL_WIDTH = LANES
DT_COL, I_COL, F_COL = 0, 32, 36
DT_ROW_COPY = 64

SCAN_CHUNK = 128
MIX_TOKENS = 256
PROJ_ROWS = 256
PROJ_STRIP = 512
MERGE_TM = 512
MERGE_ROWS = 256
HEADS_PER_DOT = 4
VMEM_LIMIT = 56 * 1024 * 1024


def _sigmoid(x):
    return 0.5 * jnp.tanh(0.5 * x) + 0.5


def _silu(x):
    h = 0.5 * x
    return h * jnp.tanh(h) + h


def _softplus(x):
    return jnp.maximum(x, 0.0) + jnp.log1p(jnp.exp(-jnp.abs(x)))


def _dot(a, b):
    return jnp.dot(a, b, preferred_element_type=F32)


def _dot_nt(a, b):
    return lax.dot_general(a, b, (((1,), (1,)), ((), ())), preferred_element_type=F32)


def _rms(x, w):
    ms = jnp.mean(x * x, axis=-1, keepdims=True)
    return x * lax.rsqrt(ms + NORM_EPS) * w


def _row_interleave_matrix():
    g = PROJ_ROWS // SUBLANES
    m = np.zeros((PROJ_ROWS, PROJ_ROWS), np.float32)
    for r in range(PROJ_ROWS):
        m[r, g * (r % SUBLANES) + r // SUBLANES] = 1.0
    return m


def _in_proj_kernel(x_ref, nw_ref, w_ref, ws_ref, cw_ref, cb_ref, pm_ref, pmt_ref,
                    main_ref, small_ref, u_ref, ui_ref):
    j = pl.program_id(1)
    seq = x_ref.shape[0]
    rc = PROJ_ROWS
    n_shift = CONV_TAPS - 1

    @pl.when(j == 0)
    def _():
        u = _rms(x_ref[...], nw_ref[...]).astype(BF16)
        u_ref[...] = u
        small_ref[...] = _dot(u, ws_ref[...])
        for p in range(seq // rc):
            rows = slice(p * rc, (p + 1) * rc)
            ui_ref[rows, :] = _dot(pm_ref[...], u[rows, :]).astype(BF16)

    is_conv = (j >= CONV_FIRST_BLOCK) & (j <= CONV_LAST_BLOCK)

    @pl.when(jnp.logical_not(is_conv))
    def _():
        main_ref[...] = _dot(u_ref[...], w_ref[...]).astype(BF16)

    @pl.when(is_conv)
    def _():
        sub0 = lax.broadcasted_iota(jnp.int32, (n_shift, SUBLANES, PROJ_STRIP), 1) == 0
        n_pieces = seq // rc
        work = [(n, p) for n in range(COL_BLOCK // PROJ_STRIP) for p in range(n_pieces)]

        def piece_dot(n, p):
            return _dot(ui_ref[p * rc:(p + 1) * rc, :], w_ref[:, n * PROJ_STRIP:(n + 1) * PROJ_STRIP])

        acc_next = piece_dot(*work[0])
        for idx, (n, p) in enumerate(work):
            rows = slice(p * rc, (p + 1) * rc)
            cols = slice(n * PROJ_STRIP, (n + 1) * PROJ_STRIP)
            acc = acc_next
            if idx + 1 < len(work):
                acc_next = piece_dot(*work[idx + 1])
            if p == 0:
                taps = [0.5 * cw_ref[t:t + 1, cols] for t in range(CONV_TAPS)]
                bias = 0.5 * cb_ref[:, cols]
                tail_prev = jnp.zeros((n_shift, SUBLANES, PROJ_STRIP), F32)
            tail = acc[rc - n_shift * SUBLANES:, :].reshape(n_shift, SUBLANES, PROJ_STRIP)
            wrapped = jnp.where(sub0, pltpu.roll(tail_prev, 1, axis=1), pltpu.roll(tail, 1, axis=1))
            wrapped = wrapped.reshape(n_shift * SUBLANES, PROJ_STRIP)
            h = bias + acc * taps[CONV_TAPS - 1]
            for t in range(n_shift):
                k = n_shift - t
                shifted = jnp.concatenate(
                    [wrapped[(n_shift - k) * SUBLANES:, :], acc[:rc - k * SUBLANES, :]], axis=0)
                h = h + shifted * taps[t]
            act = (h * jnp.tanh(h) + h).astype(BF16)
            main_ref[rows, cols] = _dot(pmt_ref[...], act).astype(BF16)
            tail_prev = tail


def _in_proj(x2, norm_w, w_main, w_small, conv_w, conv_b, seq):
    tokens = x2.shape[0]
    grid = (tokens // seq, N_COL_BLOCKS)
    interleave = _row_interleave_matrix()

    def conv_block(i, j):
        return (0, jnp.clip(j - CONV_FIRST_BLOCK, 0, CONV_LAST_BLOCK - CONV_FIRST_BLOCK))

    return pl.pallas_call(
        _in_proj_kernel,
        grid=grid,
        in_specs=[
            pl.BlockSpec((seq, D_MODEL), lambda i, j: (i, 0)),
            pl.BlockSpec((1, D_MODEL), lambda i, j: (0, 0)),
            pl.BlockSpec((D_MODEL, COL_BLOCK), lambda i, j: (0, j)),
            pl.BlockSpec((D_MODEL, SMALL_WIDTH), lambda i, j: (0, 0)),
            pl.BlockSpec((CONV_TAPS, COL_BLOCK), conv_block),
            pl.BlockSpec((1, COL_BLOCK), conv_block),
            pl.BlockSpec((PROJ_ROWS, PROJ_ROWS), lambda i, j: (0, 0)),
            pl.BlockSpec((PROJ_ROWS, PROJ_ROWS), lambda i, j: (0, 0)),
        ],
        out_specs=[
            pl.BlockSpec((seq, COL_BLOCK), lambda i, j: (i, j)),
            pl.BlockSpec((seq, SMALL_WIDTH), lambda i, j: (i, 0)),
        ],
        out_shape=[
            jax.ShapeDtypeStruct((tokens, MAIN_WIDTH), BF16),
            jax.ShapeDtypeStruct((tokens, SMALL_WIDTH), F32),
        ],
        scratch_shapes=[pltpu.VMEM((seq, D_MODEL), BF16),
                        pltpu.VMEM((seq, D_MODEL), BF16)],
        compiler_params=pltpu.CompilerParams(
            dimension_semantics=("arbitrary", "arbitrary"),
            vmem_limit_bytes=VMEM_LIMIT),
        name="in_proj",
    )(x2, norm_w, w_main, w_small, conv_w, conv_b,
      jnp.asarray(interleave, BF16), jnp.asarray(interleave.T, BF16))


def _head_expand_matrix():
    e = np.zeros((2 * LANES, SSD_INNER), np.float32)
    for h in range(SSD_HEADS):
        e[h, h * SSD_HEAD_DIM:(h + 1) * SSD_HEAD_DIM] = 1.0
        e[LANES + h, h * SSD_HEAD_DIM:(h + 1) * SSD_HEAD_DIM] = 1.0
    return jnp.asarray(e, BF16)


def _mixer_kernel(xs_ref, bc_ref, q_ref, k_ref, v_ref, sm_ref,
                  bias_ref, alog_ref, dskip_ref, e2_ref, eye_ref,
                  y_ref, hh_ref,
                  s_ref, c_ref, n_ref, m_ref, p_ref, pt_ref, dt_ref):
    tb, L = MIX_TOKENS, SCAN_CHUNK
    n_chunks = tb // L
    hp = HEADS_PER_DOT
    qw = hp * SSD_HEAD_DIM
    gw = SSD_GROUP_WIDTH

    @pl.when(pl.program_id(1) == 0)
    def _():
        s_ref[...] = jnp.zeros_like(s_ref)
        c_ref[...] = jnp.zeros_like(c_ref)
        n_ref[...] = jnp.zeros_like(n_ref)
        m_ref[...] = jnp.zeros_like(m_ref)

    sm = sm_ref[...] + bias_ref[...]
    lane = lax.broadcasted_iota(jnp.int32, (tb, SMALL_WIDTH), 1)
    dt = _softplus(sm)
    log_f = -_softplus(-sm)
    a_row = -jnp.exp(alog_ref[...])
    is_dt = lane < I_COL
    is_i = (lane >= I_COL) & (lane < F_COL)
    is_f = (lane >= F_COL) & (lane < F_COL + ML_HEADS)
    pre = jnp.where(is_dt, dt * a_row, jnp.where(is_f, log_f, 0.0))
    rt = lax.broadcasted_iota(jnp.int32, (tb, tb), 0)
    rs = lax.broadcasted_iota(jnp.int32, (tb, tb), 1)
    tri = ((rs <= rt) & ((rt // L) == (rs // L))).astype(F32)
    cs = jnp.dot(tri, pre, preferred_element_type=F32, precision=lax.Precision.HIGHEST)
    dt_copy = pltpu.roll(dt, DT_ROW_COPY, axis=1)
    is_dt_copy = (lane >= DT_ROW_COPY) & (lane < DT_ROW_COPY + SSD_HEADS)
    table = jnp.where(is_dt | is_f, cs, jnp.where(is_i, sm, jnp.where(is_dt_copy, dt_copy, 0.0)))
    p_ref[...] = table
    dt_ref[...] = jnp.where(is_dt, dt, 0.0)
    for c in range(n_chunks):
        pt_ref[c] = table[c * L:(c + 1) * L, :].T

    causal = (lax.broadcasted_iota(jnp.int32, (L, L), 0) >= lax.broadcasted_iota(jnp.int32, (L, L), 1))
    dt_lanes = lax.broadcasted_iota(jnp.int32, (L, SMALL_WIDTH), 1) < I_COL
    lane_q = lax.broadcasted_iota(jnp.int32, (1, qw), 1)
    head_masks = [(lane_q >= a * SSD_HEAD_DIM) & (lane_q < (a + 1) * SSD_HEAD_DIM) for a in range(hp)]
    neg_inf = jnp.float32(-jnp.inf)

    def split_hi_lo(v, axis=1):
        hi = v.astype(BF16)
        lo = (v - hi.astype(F32)).astype(BF16)
        return jnp.concatenate([hi, lo], axis=axis)

    for c in range(n_chunks):
        r0 = c * L
        rows = slice(r0, r0 + L)
        tab = p_ref[rows, :]
        tab_end = p_ref[r0 + L - 1:r0 + L, :]
        tab_t = pt_ref[c]
        dt_blk = dt_ref[rows, :]
        ea2 = split_hi_lo(jnp.where(dt_lanes, jnp.exp(tab), 0.0))
        td2 = split_hi_lo(jnp.where(dt_lanes, jnp.exp(tab_end - tab) * dt_blk, 0.0))

        for g in range(SSD_GROUPS):
            gcols = slice(g * gw, (g + 1) * gw)
            b_mat = bc_ref[rows, g * SSD_STATE:(g + 1) * SSD_STATE]
            c_mat = bc_ref[rows, (SSD_GROUPS + g) * SSD_STATE:(SSD_GROUPS + g + 1) * SSD_STATE]
            cb = _dot_nt(c_mat, b_mat)
            ea_g = _dot(ea2, e2_ref[:, gcols])
            td_g = _dot(td2, e2_ref[:, gcols])
            x_g = xs_ref[rows, gcols]
            x_gf = x_g.astype(F32)
            s_g = s_ref[:, gcols]
            inter = _dot(c_mat, s_g.astype(BF16)) * ea_g
            intra_parts = []
            for qd in range(gw // qw):
                x_q = x_g[:, qd * qw:(qd + 1) * qw]
                zero_x = jnp.zeros_like(x_q)
                lhs_parts, rhs_parts = [], []
                for a in range(hp):
                    h = (g * gw + qd * qw) // SSD_HEAD_DIM + a
                    seg = tab[:, h:h + 1] - tab_t[h:h + 1, :]
                    dec = jnp.exp(jnp.where(causal, seg, neg_inf))
                    w_mat = cb * dec * tab_t[DT_ROW_COPY + h:DT_ROW_COPY + h + 1, :]
                    lhs_parts.append(w_mat.astype(BF16))
                    rhs_parts.append(jnp.where(head_masks[a], x_q, zero_x))
                intra_parts.append(_dot(jnp.concatenate(lhs_parts, axis=1),
                                        jnp.concatenate(rhs_parts, axis=0)))
            y_g = jnp.concatenate(intra_parts, axis=1) + inter + dskip_ref[:, gcols] * x_gf
            y_ref[rows, gcols] = y_g.astype(BF16)
            xw = (x_gf * td_g).astype(BF16)
            b_t = _dot_nt(eye_ref[0:SSD_STATE, 0:SSD_STATE], b_mat).astype(BF16)
            s_ref[:, gcols] = s_g * ea_g[L - 1:L, :] + _dot(b_t, xw)

        heads = range(ML_HEADS)
        hcols = [slice(h * ML_HEAD_DIM, (h + 1) * ML_HEAD_DIM) for h in heads]
        q_hs = [q_ref[rows, hcols[h]] * jnp.asarray(ML_HEAD_DIM ** -0.5, BF16) for h in heads]
        k_hs = [k_ref[rows, hcols[h]] for h in heads]
        v_hs = [v_ref[rows, hcols[h]] for h in heads]
        c_prevs = [c_ref[h] for h in heads]
        n_prevs = [n_ref[h] for h in heads]
        qk_raw = [_dot_nt(q_hs[h], k_hs[h]) for h in heads]
        q_c_prev = [_dot(q_hs[h], c_prevs[h].astype(BF16)) for h in heads]
        k_ts = [_dot_nt(eye_ref[...], k_hs[h]) for h in heads]
        w_intra, w_inter, m_ts, wg2s, kw_ts, m_news, a_olds, a_locs = [], [], [], [], [], [], [], []
        for h in heads:
            b_col = tab[:, F_COL + h:F_COL + h + 1]
            b_row = tab_t[F_COL + h:F_COL + h + 1, :]
            li_row = tab_t[I_COL + h:I_COL + h + 1, :]
            b_end = tab_end[:, F_COL + h:F_COL + h + 1]
            m_prev = m_ref[0:1, h:h + 1]
            d_log = jnp.where(causal, b_col - b_row + li_row, neg_inf)
            inter_log = b_col + m_prev
            m_t = jnp.maximum(inter_log, jnp.max(d_log, axis=1, keepdims=True))
            w_intra.append(jnp.exp(d_log - m_t))
            w_inter.append(jnp.exp(inter_log - m_t))
            m_ts.append(m_t)
            g_row = b_end - b_row + li_row
            m_loc = jnp.max(g_row, axis=1, keepdims=True)
            wg_row = jnp.exp(g_row - m_loc)
            kw_ts.append((k_ts[h] * wg_row).astype(BF16))
            wg2s.append(split_hi_lo(wg_row))
            m_new = jnp.maximum(b_end + m_prev, m_loc)
            m_news.append(m_new)
            a_olds.append(jnp.exp(b_end + m_prev - m_new))
            a_locs.append(jnp.exp(m_loc - m_new))
        c_locs = [_dot(kw_ts[h], v_hs[h]) for h in heads]
        n_locs = [_dot(wg2s[h], jnp.concatenate([k_hs[h], k_hs[h]], axis=0)) for h in heads]
        qks = [qk_raw[h] * w_intra[h] for h in heads]
        nums = [_dot(qks[h].astype(BF16), v_hs[h]) + w_inter[h] * q_c_prev[h] for h in heads]
        for h in heads:
            den = (jnp.sum(qks[h], axis=1, keepdims=True)
                   + w_inter[h] * jnp.sum(q_hs[h].astype(F32) * n_prevs[h], axis=1, keepdims=True))
            hh = nums[h] / jnp.maximum(jnp.abs(den), jnp.exp(-m_ts[h]))
            hh_ref[rows, hcols[h]] = hh.astype(BF16)
        for h in heads:
            c_ref[h] = a_olds[h] * c_prevs[h] + a_locs[h] * c_locs[h]
            n_ref[h] = a_olds[h] * n_prevs[h] + a_locs[h] * n_locs[h]
            m_ref[0:1, h:h + 1] = m_news[h]


def _mixers(main, small, bias_row, alog_row, dskip_row, batch, seq):
    tb, L = MIX_TOKENS, SCAN_CHUNK
    spb = seq // tb
    tokens = batch * seq
    e2 = _head_expand_matrix()
    eye = jnp.eye(ML_HEAD_DIM, dtype=BF16)

    def col(width, idx):
        return pl.BlockSpec((tb, width), lambda b, s: (b * spb + s, idx))

    def const(shape):
        return pl.BlockSpec(shape, lambda b, s: tuple(0 for _ in shape))

    in_specs = [
        col(2 * COL_BLOCK, 1),
        col(COL_BLOCK, 4),
        col(COL_BLOCK, 5),
        col(COL_BLOCK, 6),
        col(COL_BLOCK, 7),
        pl.BlockSpec((tb, SMALL_WIDTH), lambda b, s: (b * spb + s, 0)),
        const((1, SMALL_WIDTH)), const((1, SMALL_WIDTH)), const((1, SSD_INNER)),
        const(e2.shape), const(eye.shape),
    ]
    out_specs = [
        pl.BlockSpec((tb, SSD_INNER), lambda b, s: (b * spb + s, 0)),
        pl.BlockSpec((tb, ML_INNER), lambda b, s: (b * spb + s, 0)),
    ]
    scratch = [
        pltpu.VMEM((SSD_STATE, SSD_INNER), F32),
        pltpu.VMEM((ML_HEADS, ML_HEAD_DIM, ML_HEAD_DIM), F32),
        pltpu.VMEM((ML_HEADS, 1, ML_HEAD_DIM), F32),
        pltpu.VMEM((SUBLANES, LANES), F32),
        pltpu.VMEM((tb, SMALL_WIDTH), F32),
        pltpu.VMEM((tb // L, SMALL_WIDTH, L), F32),
        pltpu.VMEM((tb, SMALL_WIDTH), F32),
    ]
    return pl.pallas_call(
        _mixer_kernel,
        grid=(batch, spb),
        in_specs=in_specs,
        out_specs=out_specs,
        out_shape=[jax.ShapeDtypeStruct((tokens, SSD_INNER), BF16),
                   jax.ShapeDtypeStruct((tokens, ML_INNER), BF16)],
        scratch_shapes=scratch,
        compiler_params=pltpu.CompilerParams(
            dimension_semantics=("arbitrary", "arbitrary"),
            vmem_limit_bytes=VMEM_LIMIT),
        name="mixers",
    )(main, main, main, main, main, small, bias_row, alog_row, dskip_row, e2, eye)


def _merge_kernel(x_ref, y_ref, hh_ref, z_ref, o_ref, gs_ref, gm_ref, snw_ref, mnw_ref,
                  wbs_ref, wbm_ref, wo_ref, nmw_ref, wup_ref, wdn_ref, nfw_ref, out_ref):
    for r0 in range(0, x_ref.shape[0], MERGE_ROWS):
        rows = slice(r0, r0 + MERGE_ROWS)
        gw = SSD_GROUP_WIDTH
        y_parts = []
        for g in range(SSD_GROUPS):
            cols = slice(g * gw, (g + 1) * gw)
            yz = y_ref[rows, cols].astype(F32) * _silu(z_ref[rows, cols].astype(F32))
            y_parts.append(_rms(yz, snw_ref[:, cols]).astype(BF16))
        y_n = jnp.concatenate(y_parts, axis=1)
        h_parts = []
        for h in range(ML_HEADS):
            cols = slice(h * ML_HEAD_DIM, (h + 1) * ML_HEAD_DIM)
            hn = _rms(hh_ref[rows, cols].astype(F32), mnw_ref[:, cols])
            h_parts.append((_sigmoid(o_ref[rows, cols].astype(F32)) * hn).astype(BF16))
        hm = jnp.concatenate(h_parts, axis=1)

        a = _dot(y_n, wbs_ref[...])
        b = _dot(hm, wbm_ref[...])
        mixed = (_sigmoid(gs_ref[rows, :].astype(F32)) * a + _sigmoid(gm_ref[rows, :].astype(F32)) * b)
        h1 = x_ref[rows, :] + _dot(mixed.astype(BF16), wo_ref[...])
        u = _rms(h1, nmw_ref[...]).astype(BF16)
        up = jnp.maximum(_dot(u, wup_ref[...]), 0.0)
        act = (up * up).astype(BF16)
        h2 = h1 + _dot(act, wdn_ref[...])
        out_ref[rows, :] = _rms(h2, nfw_ref[...])


def _merge(x2, y, hh, main, ssd_nw, ml_nw, w_br_ssd, w_br_ml, w_out, norm_mlp_w, w_up, w_down, norm_final_w):
    tokens = x2.shape[0]
    tm = MERGE_TM

    def resident(shape):
        return pl.BlockSpec(shape, lambda i: (0, 0), pipeline_mode=pl.Buffered(1))

    return pl.pallas_call(
        _merge_kernel,
        grid=(tokens // tm,),
        in_specs=[
            pl.BlockSpec((tm, D_MODEL), lambda i: (i, 0)),
            pl.BlockSpec((tm, SSD_INNER), lambda i: (i, 0)),
            pl.BlockSpec((tm, ML_INNER), lambda i: (i, 0)),
            pl.BlockSpec((tm, 2 * COL_BLOCK), lambda i: (i, 0)),
            pl.BlockSpec((tm, COL_BLOCK), lambda i: (i, 8)),
            pl.BlockSpec((tm, COL_BLOCK), lambda i: (i, 9)),
            pl.BlockSpec((tm, COL_BLOCK), lambda i: (i, 10)),
            resident((1, SSD_INNER)),
            resident((1, ML_INNER)),
            resident((SSD_INNER, D_MODEL)),
            resident((ML_INNER, D_MODEL)),
            resident((D_MODEL, D_MODEL)),
            resident((1, D_MODEL)),
            resident((D_MODEL, D_FF)),
            resident((D_FF, D_MODEL)),
            resident((1, D_MODEL)),
        ],
        out_specs=pl.BlockSpec((tm, D_MODEL), lambda i: (i, 0)),
        out_shape=jax.ShapeDtypeStruct((tokens, D_MODEL), F32),
        compiler_params=pltpu.CompilerParams(
            dimension_semantics=("arbitrary",),
            vmem_limit_bytes=VMEM_LIMIT),
        name="merge_mlp",
    )(x2, y, hh, main, main, main, main, ssd_nw, ml_nw,
      w_br_ssd, w_br_ml, w_out, norm_mlp_w, w_up, w_down, norm_final_w)


def kernel(x, norm_mix_w, w_in, conv_ssd_w, conv_ssd_b, dt_bias, a_log, d_skip, ssd_norm_w,
           conv_qk_w, conv_qk_b, i_bias, f_bias, mlstm_norm_w, w_br_ssd, w_br_mlstm, w_out,
           norm_mlp_w, w_up, w_down, norm_final_w):
    batch, seq, _ = x.shape
    x2 = x.reshape(batch * seq, D_MODEL)
    layer = 0

    w = w_in[layer]
    o_xbc = SSD_INNER
    o_dt = o_xbc + SSD_INNER + 2 * SSD_GROUPS * SSD_STATE
    o_q = o_dt + SSD_HEADS
    o_i = o_q + 4 * ML_INNER
    o_g = o_i + 2 * ML_HEADS
    w_main = jnp.concatenate([w[:, :o_dt], w[:, o_q:o_i], w[:, o_g:o_g + 2 * D_MODEL]], axis=1).astype(BF16)
    pad = SMALL_WIDTH - SSD_HEADS - 2 * ML_HEADS
    w_small = jnp.concatenate([w[:, o_dt:o_q], w[:, o_i:o_g], jnp.zeros((D_MODEL, pad), F32)], axis=1).astype(BF16)
    conv_w = jnp.concatenate([conv_ssd_w[layer], conv_qk_w[layer]], axis=1)
    conv_b = jnp.concatenate([conv_ssd_b[layer], conv_qk_b[layer]]).reshape(1, CONV_WIDTH)

    main, small = _in_proj(x2, norm_mix_w[layer].reshape(1, D_MODEL), w_main, w_small, conv_w, conv_b, seq)

    zeros = jnp.zeros((SMALL_WIDTH - F_COL - ML_HEADS,), F32)
    bias_row = jnp.concatenate([dt_bias[layer], i_bias[layer], f_bias[layer], zeros]).reshape(1, SMALL_WIDTH)
    alog_row = jnp.concatenate([a_log[layer], jnp.zeros((SMALL_WIDTH - SSD_HEADS,), F32)]).reshape(1, SMALL_WIDTH)
    dskip_row = jnp.repeat(d_skip[layer].astype(F32), SSD_HEAD_DIM).reshape(1, SSD_INNER)

    y, hh = _mixers(main, small, bias_row, alog_row, dskip_row, batch, seq)

    out = _merge(x2, y, hh, main,
                 ssd_norm_w[layer].reshape(1, SSD_INNER), mlstm_norm_w[layer].reshape(1, ML_INNER),
                 w_br_ssd[layer].astype(BF16), w_br_mlstm[layer].astype(BF16), w_out[layer].astype(BF16),
                 norm_mlp_w[layer].reshape(1, D_MODEL), w_up[layer].astype(BF16), w_down[layer].astype(BF16),
                 norm_final_w.reshape(1, D_MODEL))
    return out.reshape(batch, seq, D_MODEL)
```

```python
import functools

import jax
import jax.numpy as jnp
import numpy as np
from jax import lax
from jax.experimental import pallas as pl
from jax.experimental.pallas import tpu as pltpu

F32 = jnp.float32
BF16 = jnp.bfloat16

D_MODEL = 1024
NORM_EPS = 1e-5
CONV_TAPS = 4
SSD_INNER = 2048
SSD_HEAD_DIM = 64
SSD_HEADS = 32
SSD_GROUPS = 4
SSD_STATE = 128
SSD_GROUP_WIDTH = SSD_INNER // SSD_GROUPS
SSD_XBC = SSD_INNER + 2 * SSD_GROUPS * SSD_STATE
ML_INNER = 1024
ML_HEADS = 4
ML_HEAD_DIM = 256
D_FF = 4096
IN_PROJ_WIDTH = SSD_INNER + SSD_XBC + SSD_HEADS + 4 * ML_INNER + 2 * ML_HEADS + 2 * D_MODEL

LANES = 128
BF16_ROWS = 16

COL_BLOCK = 1024
N_COL_BLOCKS = 11
MAIN_WIDTH = N_COL_BLOCKS * COL_BLOCK
Q_FIRST_BLOCK, G_FIRST_BLOCK = 5, 9
Q_LANE_OFFSET = SSD_HEADS
G_LANE_OFFSET = SSD_HEADS + 2 * ML_HEADS
SMALL_WIDTH = LANES
DT_COL, I_COL, F_COL = 0, 32, 36
DT_ROW_COPY = 64

CONV_WIDTH = SSD_XBC + 2 * ML_INNER
SCAN_CHUNK = 128
MIX_TOKENS = 256
CONV_PIECES = (112, 112, 32)
CONV_STRIP = 256
PROJ_TM = 2048
MERGE_TM = 512
HEADS_PER_DOT = 4
VMEM_LIMIT = 56 * 1024 * 1024


def _sigmoid(x):
    return 0.5 * jnp.tanh(0.5 * x) + 0.5


def _silu(x):
    h = 0.5 * x
    return h * jnp.tanh(h) + h


def _softplus(x):
    return jnp.maximum(x, 0.0) + jnp.log1p(jnp.exp(-jnp.abs(x)))


def _dot(a, b):
    return jnp.dot(a, b, preferred_element_type=F32)


def _dot_nt(a, b):
    return lax.dot_general(a, b, (((1,), (1,)), ((), ())), preferred_element_type=F32)


def _w_prep_kernel(wa_ref, wb_ref, out_ref):
    j = pl.program_id(0)

    def shifted(off):
        both = jnp.concatenate([wa_ref[...], wb_ref[...]], axis=1)
        return pltpu.roll(both, 2 * COL_BLOCK - off, axis=1)[:, :COL_BLOCK]

    @pl.when(j < Q_FIRST_BLOCK)
    def _():
        out_ref[...] = wa_ref[...].astype(BF16)

    @pl.when((j >= Q_FIRST_BLOCK) & (j < G_FIRST_BLOCK))
    def _():
        out_ref[...] = shifted(Q_LANE_OFFSET).astype(BF16)

    @pl.when(j >= G_FIRST_BLOCK)
    def _():
        out_ref[...] = shifted(G_LANE_OFFSET).astype(BF16)


def _w_prep(w):
    last = pl.cdiv(IN_PROJ_WIDTH, COL_BLOCK) - 1
    return pl.pallas_call(
        _w_prep_kernel,
        grid=(N_COL_BLOCKS,),
        in_specs=[
            pl.BlockSpec((D_MODEL, COL_BLOCK), lambda j: (0, j)),
            pl.BlockSpec((D_MODEL, COL_BLOCK), lambda j: (0, jnp.minimum(j + 1, last))),
        ],
        out_specs=pl.BlockSpec((D_MODEL, COL_BLOCK), lambda j: (0, j)),
        out_shape=jax.ShapeDtypeStruct((D_MODEL, MAIN_WIDTH), BF16),
        compiler_params=pltpu.CompilerParams(
            dimension_semantics=("arbitrary",),
            vmem_limit_bytes=VMEM_LIMIT),
        name="w_prep",
    )(w, w)


def _in_proj_kernel(x_ref, nw_ref, w_ref, ws_ref, main_ref, small_ref, u_ref):
    j = pl.program_id(1)

    @pl.when(j == 0)
    def _():
        x = x_ref[...]
        ms = jnp.mean(x * x, axis=-1, keepdims=True)
        u = (x * lax.rsqrt(ms + NORM_EPS) * nw_ref[...]).astype(BF16)
        u_ref[...] = u
        small_ref[...] = _dot(u, ws_ref[...])

    main_ref[...] = _dot(u_ref[...], w_ref[...]).astype(BF16)


def _in_proj(x2, norm_w, w_main, w_small):
    tokens = x2.shape[0]
    grid = (tokens // PROJ_TM, N_COL_BLOCKS)
    return pl.pallas_call(
        _in_proj_kernel,
        grid=grid,
        in_specs=[
            pl.BlockSpec((PROJ_TM, D_MODEL), lambda i, j: (i, 0)),
            pl.BlockSpec((1, D_MODEL), lambda i, j: (0, 0)),
            pl.BlockSpec((D_MODEL, COL_BLOCK), lambda i, j: (0, j)),
            pl.BlockSpec((D_MODEL, SMALL_WIDTH), lambda i, j: (0, 0)),
        ],
        out_specs=[
            pl.BlockSpec((PROJ_TM, COL_BLOCK), lambda i, j: (i, j)),
            pl.BlockSpec((PROJ_TM, SMALL_WIDTH), lambda i, j: (i, 0)),
        ],
        out_shape=[
            jax.ShapeDtypeStruct((tokens, MAIN_WIDTH), BF16),
            jax.ShapeDtypeStruct((tokens, SMALL_WIDTH), F32),
        ],
        scratch_shapes=[pltpu.VMEM((PROJ_TM, D_MODEL), BF16)],
        compiler_params=pltpu.CompilerParams(
            dimension_semantics=("arbitrary", "arbitrary"),
            vmem_limit_bytes=VMEM_LIMIT),
        name="in_proj",
    )(x2, norm_w, w_main, w_small)


def _shift_select_matrix(p):
    h = BF16_ROWS
    sel = np.zeros((p, CONV_TAPS * (p + h)), np.float32)
    for tap in range(CONV_TAPS):
        for t in range(p):
            sel[t, tap * (p + h) + t + h - (CONV_TAPS - 1 - tap)] = 1.0
    return jnp.asarray(sel, BF16)


def _head_expand_matrix():
    e = np.zeros((2 * LANES, SSD_INNER), np.float32)
    for h in range(SSD_HEADS):
        e[h, h * SSD_HEAD_DIM:(h + 1) * SSD_HEAD_DIM] = 1.0
        e[LANES + h, h * SSD_HEAD_DIM:(h + 1) * SSD_HEAD_DIM] = 1.0
    return jnp.asarray(e, BF16)


def _mixer_kernel(n_blocks, z_ref, xs_ref, bc_ref, q_ref, k_ref, v_ref, o_ref, sm_ref,
                  cw_ssd_ref, cb_ssd_ref, cw_qk_ref, cb_qk_ref,
                  bias_ref, alog_ref, dskip_ref, ssd_nw_ref, ml_nw_ref, sel_a_ref, sel_b_ref, e2_ref, eye_ref,
                  y_ref, hm_ref,
                  hist_ref, conv0_ref, conv1_ref,
                  s_ref, c_ref, n_ref, m_ref, p_ref, pt_ref, dt_ref):
    tb, L = MIX_TOKENS, SCAN_CHUNK
    n_chunks = tb // L
    hp = HEADS_PER_DOT
    qw = hp * SSD_HEAD_DIM
    gw = SSD_GROUP_WIDTH
    step = pl.program_id(1)
    bc_off, q_off, k_off = SSD_INNER, SSD_XBC, SSD_XBC + ML_INNER

    @pl.when(step == 0)
    def _():
        hist_ref[...] = jnp.zeros_like(hist_ref)
        s_ref[...] = jnp.zeros_like(s_ref)
        c_ref[...] = jnp.zeros_like(c_ref)
        n_ref[...] = jnp.zeros_like(n_ref)
        m_ref[...] = jnp.zeros_like(m_ref)

    def conv_items(dst_ref):
        h = BF16_ROWS
        sels = {CONV_PIECES[0]: sel_a_ref, CONV_PIECES[-1]: sel_b_ref}
        sources = ((xs_ref, 0, SSD_INNER, cw_ssd_ref, cb_ssd_ref, 0),
                   (bc_ref, bc_off, SSD_XBC - SSD_INNER, cw_ssd_ref, cb_ssd_ref, SSD_INNER),
                   (q_ref, q_off, ML_INNER, cw_qk_ref, cb_qk_ref, 0),
                   (k_ref, k_off, ML_INNER, cw_qk_ref, cb_qk_ref, ML_INNER))

        def strip(src_ref, dst_off, w_ref, b_ref, w_off, c0):
            cols = slice(c0, c0 + CONV_STRIP)
            wcols = slice(w_off + c0, w_off + c0 + CONV_STRIP)
            dcols = slice(dst_off + c0, dst_off + c0 + CONV_STRIP)
            taps = [w_ref[tap:tap + 1, wcols].astype(BF16) for tap in range(CONV_TAPS)]
            r0 = 0
            for p in CONV_PIECES:
                if r0 == 0:
                    win = jnp.concatenate([hist_ref[:, dcols], src_ref[0:p, cols]], axis=0)
                else:
                    win = src_ref[r0 - h:r0 + p, cols]
                scaled = jnp.concatenate([win * taps[tap] for tap in range(CONV_TAPS)], axis=0)
                acc = _dot(sels[p][...], scaled) + b_ref[:, wcols]
                dst_ref[r0:r0 + p, dcols] = _silu(acc).astype(BF16)
                r0 += p
            hist_ref[:, dcols] = src_ref[tb - h:tb, cols]

        items = []
        for src_ref, dst_off, width, w_ref, b_ref, w_off in sources:
            for c0 in range(0, width, CONV_STRIP):
                items.append(lambda a=(src_ref, dst_off, w_ref, b_ref, w_off, c0): strip(*a))
        return items

    def scans(cv_ref, pending):
        pending = list(pending)
        n_slots = n_chunks * (SSD_GROUPS + 3)
        per_slot = -(-len(pending) // n_slots)

        def emit():
            for _ in range(per_slot):
                if pending:
                    pending.pop(0)()

        sm = sm_ref[...] + bias_ref[...]
        lane = lax.broadcasted_iota(jnp.int32, (tb, SMALL_WIDTH), 1)
        dt = _softplus(sm)
        log_f = -_softplus(-sm)
        a_row = -jnp.exp(alog_ref[...])
        is_dt = lane < I_COL
        is_i = (lane >= I_COL) & (lane < F_COL)
        is_f = (lane >= F_COL) & (lane < F_COL + ML_HEADS)
        pre = jnp.where(is_dt, dt * a_row, jnp.where(is_f, log_f, 0.0))
        rt = lax.broadcasted_iota(jnp.int32, (tb, tb), 0)
        rs = lax.broadcasted_iota(jnp.int32, (tb, tb), 1)
        tri = ((rs <= rt) & ((rt // L) == (rs // L))).astype(F32)
        cs = jnp.dot(tri, pre, preferred_element_type=F32, precision=lax.Precision.HIGHEST)
        dt_copy = pltpu.roll(dt, DT_ROW_COPY, axis=1)
        is_dt_copy = (lane >= DT_ROW_COPY) & (lane < DT_ROW_COPY + SSD_HEADS)
        table = jnp.where(is_dt | is_f, cs, jnp.where(is_i, sm, jnp.where(is_dt_copy, dt_copy, 0.0)))
        p_ref[...] = table
        dt_ref[...] = jnp.where(is_dt, dt, 0.0)
        for c in range(n_chunks):
            pt_ref[c] = table[c * L:(c + 1) * L, :].T

        causal = (lax.broadcasted_iota(jnp.int32, (L, L), 0) >= lax.broadcasted_iota(jnp.int32, (L, L), 1))
        dt_lanes = lax.broadcasted_iota(jnp.int32, (L, SMALL_WIDTH), 1) < I_COL
        lane_q = lax.broadcasted_iota(jnp.int32, (1, qw), 1)
        head_masks = [(lane_q >= a * SSD_HEAD_DIM) & (lane_q < (a + 1) * SSD_HEAD_DIM) for a in range(hp)]
        neg_inf = jnp.float32(-jnp.inf)

        def split_hi_lo(v):
            hi = v.astype(BF16)
            lo = (v - hi.astype(F32)).astype(BF16)
            return jnp.concatenate([hi, lo], axis=1)

        for c in range(n_chunks):
            r0 = c * L
            rows = slice(r0, r0 + L)
            tab = p_ref[rows, :]
            tab_end = p_ref[r0 + L - 1:r0 + L, :]
            tab_t = pt_ref[c]
            dt_blk = dt_ref[rows, :]
            ea2 = split_hi_lo(jnp.where(dt_lanes, jnp.exp(tab), 0.0))
            td2 = split_hi_lo(jnp.where(dt_lanes, jnp.exp(tab_end - tab) * dt_blk, 0.0))

            for g in range(SSD_GROUPS):
                emit()
                gcols = slice(g * gw, (g + 1) * gw)
                b_mat = cv_ref[rows, bc_off + g * SSD_STATE:bc_off + (g + 1) * SSD_STATE]
                c_mat = cv_ref[rows, bc_off + (SSD_GROUPS + g) * SSD_STATE:
                               bc_off + (SSD_GROUPS + g + 1) * SSD_STATE]
                cb = _dot_nt(c_mat, b_mat)
                ea_g = _dot(ea2, e2_ref[:, gcols])
                td_g = _dot(td2, e2_ref[:, gcols])
                x_g = cv_ref[rows, gcols]
                x_gf = x_g.astype(F32)
                s_g = s_ref[:, gcols]
                inter = _dot(c_mat, s_g.astype(BF16)) * ea_g
                intra_parts = []
                for qd in range(gw // qw):
                    x_q = x_g[:, qd * qw:(qd + 1) * qw]
                    zero_x = jnp.zeros_like(x_q)
                    lhs_parts, rhs_parts = [], []
                    for a in range(hp):
                        h = (g * gw + qd * qw) // SSD_HEAD_DIM + a
                        seg = tab[:, h:h + 1] - tab_t[h:h + 1, :]
                        dec = jnp.exp(jnp.where(causal, seg, neg_inf))
                        w_mat = cb * dec * tab_t[DT_ROW_COPY + h:DT_ROW_COPY + h + 1, :]
                        lhs_parts.append(w_mat.astype(BF16))
                        rhs_parts.append(jnp.where(head_masks[a], x_q, zero_x))
                    intra_parts.append(_dot(jnp.concatenate(lhs_parts, axis=1),
                                            jnp.concatenate(rhs_parts, axis=0)))
                y_g = jnp.concatenate(intra_parts, axis=1) + inter + dskip_ref[:, gcols] * x_gf
                yz = y_g * _silu(z_ref[rows, gcols].astype(F32))
                ms = jnp.mean(yz * yz, axis=-1, keepdims=True)
                y_ref[rows, gcols] = (yz * lax.rsqrt(ms + NORM_EPS) * ssd_nw_ref[:, gcols]).astype(BF16)
                xw = (x_gf * td_g).astype(BF16)
                b_t = _dot_nt(eye_ref[0:SSD_STATE, 0:SSD_STATE], b_mat).astype(BF16)
                s_ref[:, gcols] = s_g * ea_g[L - 1:L, :] + _dot(b_t, xw)

            emit()
            heads = range(ML_HEADS)
            hcols = [slice(h * ML_HEAD_DIM, (h + 1) * ML_HEAD_DIM) for h in heads]
            q_hs = [cv_ref[rows, q_off + h * ML_HEAD_DIM:q_off + (h + 1) * ML_HEAD_DIM]
                    * jnp.asarray(ML_HEAD_DIM ** -0.5, BF16) for h in heads]
            k_hs = [cv_ref[rows, k_off + h * ML_HEAD_DIM:k_off + (h + 1) * ML_HEAD_DIM] for h in heads]
            v_hs = [v_ref[rows, hcols[h]] for h in heads]
            c_prevs = [c_ref[h] for h in heads]
            n_prevs = [n_ref[h] for h in heads]
            qk_raw = [_dot_nt(q_hs[h], k_hs[h]) for h in heads]
            q_c_prev = [_dot(q_hs[h], c_prevs[h].astype(BF16)) for h in heads]
            k_ts = [_dot_nt(eye_ref[...], k_hs[h]) for h in heads]
            w_intra, w_inter, m_ts, wg2s, kw_ts, m_news, a_olds, a_locs = [], [], [], [], [], [], [], []
            for h in heads:
                b_col = tab[:, F_COL + h:F_COL + h + 1]
                b_row = tab_t[F_COL + h:F_COL + h + 1, :]
                li_row = tab_t[I_COL + h:I_COL + h + 1, :]
                b_end = tab_end[:, F_COL + h:F_COL + h + 1]
                m_prev = m_ref[0:1, h:h + 1]
                d_log = jnp.where(causal, b_col - b_row + li_row, neg_inf)
                inter_log = b_col + m_prev
                m_t = jnp.maximum(inter_log, jnp.max(d_log, axis=1, keepdims=True))
                w_intra.append(jnp.exp(d_log - m_t))
                w_inter.append(jnp.exp(inter_log - m_t))
                m_ts.append(m_t)
                g_row = b_end - b_row + li_row
                m_loc = jnp.max(g_row, axis=1, keepdims=True)
                wg_row = jnp.exp(g_row - m_loc)
                kw_ts.append((k_ts[h] * wg_row).astype(BF16))
                wg2s.append(split_hi_lo(wg_row))
                m_new = jnp.maximum(b_end + m_prev, m_loc)
                m_news.append(m_new)
                a_olds.append(jnp.exp(b_end + m_prev - m_new))
                a_locs.append(jnp.exp(m_loc - m_new))
            emit()
            c_locs = [_dot(kw_ts[h], v_hs[h]) for h in heads]
            n_locs = [_dot(wg2s[h], jnp.concatenate([k_hs[h], k_hs[h]], axis=0)) for h in heads]
            qks = [qk_raw[h] * w_intra[h] for h in heads]
            nums = [_dot(qks[h].astype(BF16), v_hs[h]) + w_inter[h] * q_c_prev[h] for h in heads]
            emit()
            for h in heads:
                den = (jnp.sum(qks[h], axis=1, keepdims=True)
                       + w_inter[h] * jnp.sum(q_hs[h].astype(F32) * n_prevs[h], axis=1, keepdims=True))
                hh = nums[h] / jnp.maximum(jnp.abs(den), jnp.exp(-m_ts[h]))
                ms = jnp.mean(hh * hh, axis=-1, keepdims=True)
                hn = hh * lax.rsqrt(ms + NORM_EPS) * ml_nw_ref[:, hcols[h]]
                hm_ref[rows, hcols[h]] = (_sigmoid(o_ref[rows, hcols[h]].astype(F32)) * hn).astype(BF16)
            for h in heads:
                c_ref[h] = a_olds[h] * c_prevs[h] + a_locs[h] * c_locs[h]
                n_ref[h] = a_olds[h] * n_prevs[h] + a_locs[h] * n_locs[h]
                m_ref[0:1, h:h + 1] = m_news[h]
        while pending:
            pending.pop(0)()

    bufs = (conv0_ref, conv1_ref)

    @pl.when(step == 0)
    def _():
        for item in conv_items(bufs[0]):
            item()

    for parity in range(2):
        @pl.when((step > 0) & (step < n_blocks) & (step % 2 == parity))
        def _():
            scans(bufs[1 - parity], conv_items(bufs[parity]))

    @pl.when(step == n_blocks)
    def _():
        scans(bufs[(n_blocks - 1) % 2], [])


def _mixers(main, small, cw_ssd, cb_ssd, cw_qk, cb_qk,
            bias_row, alog_row, dskip_row, ssd_nw, ml_nw, batch, seq):
    tb, L = MIX_TOKENS, SCAN_CHUNK
    spb = seq // tb
    tokens = batch * seq
    assert sum(CONV_PIECES) == tb and len(set(CONV_PIECES)) == 2
    sel_a = _shift_select_matrix(CONV_PIECES[0])
    sel_b = _shift_select_matrix(CONV_PIECES[-1])
    e2 = _head_expand_matrix()
    eye = jnp.eye(ML_HEAD_DIM, dtype=BF16)

    def ahead(width, idx):
        return pl.BlockSpec((tb, width), lambda b, s: (b * spb + jnp.minimum(s, spb - 1), idx))

    def behind(width, idx):
        return pl.BlockSpec((tb, width), lambda b, s: (b * spb + jnp.maximum(s - 1, 0), idx))

    def const(shape):
        return pl.BlockSpec(shape, lambda b, s: tuple(0 for _ in shape))

    in_specs = [
        behind(2 * COL_BLOCK, 0),
        ahead(2 * COL_BLOCK, 1),
        ahead(COL_BLOCK, 4),
        ahead(COL_BLOCK, 5),
        ahead(COL_BLOCK, 6),
        behind(COL_BLOCK, 7),
        behind(COL_BLOCK, 8),
        behind(SMALL_WIDTH, 0),
        const((CONV_TAPS, SSD_XBC)), const((1, SSD_XBC)),
        const((CONV_TAPS, 2 * ML_INNER)), const((1, 2 * ML_INNER)),
        const((1, SMALL_WIDTH)), const((1, SMALL_WIDTH)),
        const((1, SSD_INNER)), const((1, SSD_INNER)), const((1, ML_INNER)),
        const(sel_a.shape), const(sel_b.shape), const(e2.shape), const(eye.shape),
    ]
    out_specs = [behind(SSD_INNER, 0), behind(ML_INNER, 0)]
    scratch = [
        pltpu.VMEM((BF16_ROWS, CONV_WIDTH), BF16),
        pltpu.VMEM((tb, CONV_WIDTH), BF16),
        pltpu.VMEM((tb, CONV_WIDTH), BF16),
        pltpu.VMEM((SSD_STATE, SSD_INNER), F32),
        pltpu.VMEM((ML_HEADS, ML_HEAD_DIM, ML_HEAD_DIM), F32),
        pltpu.VMEM((ML_HEADS, 1, ML_HEAD_DIM), F32),
        pltpu.VMEM((8, LANES), F32),
        pltpu.VMEM((tb, SMALL_WIDTH), F32),
        pltpu.VMEM((tb // L, SMALL_WIDTH, L), F32),
        pltpu.VMEM((tb, SMALL_WIDTH), F32),
    ]
    return pl.pallas_call(
        functools.partial(_mixer_kernel, spb),
        grid=(batch, spb + 1),
        in_specs=in_specs,
        out_specs=out_specs,
        out_shape=[jax.ShapeDtypeStruct((tokens, SSD_INNER), BF16),
                   jax.ShapeDtypeStruct((tokens, ML_INNER), BF16)],
        scratch_shapes=scratch,
        compiler_params=pltpu.CompilerParams(
            dimension_semantics=("arbitrary", "arbitrary"),
            vmem_limit_bytes=VMEM_LIMIT),
        name="mixers",
    )(main, main, main, main, main, main, main, small,
      cw_ssd, cb_ssd, cw_qk, cb_qk,
      bias_row, alog_row, dskip_row, ssd_nw, ml_nw, sel_a, sel_b, e2, eye)


def _rms(x, w):
    ms = jnp.mean(x * x, axis=-1, keepdims=True)
    return x * lax.rsqrt(ms + NORM_EPS) * w


def _merge_kernel(x_ref, y_ref, hm_ref, gs_ref, gm_ref,
                  wbs_ref, wbm_ref, wo_ref, nmw_ref, wup_ref, wdn_ref, nfw_ref, out_ref):
    a = _dot(y_ref[...], wbs_ref[...])
    b = _dot(hm_ref[...], wbm_ref[...])
    mixed = _sigmoid(gs_ref[...].astype(F32)) * a + _sigmoid(gm_ref[...].astype(F32)) * b
    h1 = x_ref[...] + _dot(mixed.astype(BF16), wo_ref[...])
    u = _rms(h1, nmw_ref[...]).astype(BF16)
    up = jnp.maximum(_dot(u, wup_ref[...]), 0.0)
    act = (up * up).astype(BF16)
    h2 = h1 + _dot(act, wdn_ref[...])
    out_ref[...] = _rms(h2, nfw_ref[...])


def _merge(x2, y, hm, main, w_br_ssd, w_br_ml, w_out, norm_mlp_w, w_up, w_down, norm_final_w):
    tokens = x2.shape[0]
    tm = MERGE_TM

    def resident(shape):
        return pl.BlockSpec(shape, lambda i: (0, 0), pipeline_mode=pl.Buffered(1))

    return pl.pallas_call(
        _merge_kernel,
        grid=(tokens // tm,),
        in_specs=[
            pl.BlockSpec((tm, D_MODEL), lambda i: (i, 0)),
            pl.BlockSpec((tm, SSD_INNER), lambda i: (i, 0)),
            pl.BlockSpec((tm, ML_INNER), lambda i: (i, 0)),
            pl.BlockSpec((tm, COL_BLOCK), lambda i: (i, 9)),
            pl.BlockSpec((tm, COL_BLOCK), lambda i: (i, 10)),
            resident((SSD_INNER, D_MODEL)),
            resident((ML_INNER, D_MODEL)),
            resident((D_MODEL, D_MODEL)),
            resident((1, D_MODEL)),
            resident((D_MODEL, D_FF)),
            resident((D_FF, D_MODEL)),
            resident((1, D_MODEL)),
        ],
        out_specs=pl.BlockSpec((tm, D_MODEL), lambda i: (i, 0)),
        out_shape=jax.ShapeDtypeStruct((tokens, D_MODEL), F32),
        compiler_params=pltpu.CompilerParams(
            dimension_semantics=("arbitrary",),
            vmem_limit_bytes=VMEM_LIMIT),
        name="merge_mlp",
    )(x2, y, hm, main, main, w_br_ssd, w_br_ml, w_out, norm_mlp_w, w_up, w_down, norm_final_w)


def kernel(x, norm_mix_w, w_in, conv_ssd_w, conv_ssd_b, dt_bias, a_log, d_skip, ssd_norm_w,
           conv_qk_w, conv_qk_b, i_bias, f_bias, mlstm_norm_w, w_br_ssd, w_br_mlstm, w_out,
           norm_mlp_w, w_up, w_down, norm_final_w):
    batch, seq, _ = x.shape
    x2 = x.reshape(batch * seq, D_MODEL)
    layer = 0

    w = w_in[layer]
    w_main = _w_prep(w)
    o_dt = SSD_INNER + SSD_XBC
    o_i = o_dt + SSD_HEADS + 4 * ML_INNER
    pad = SMALL_WIDTH - SSD_HEADS - 2 * ML_HEADS
    w_small = jnp.concatenate([w[:, o_dt:o_dt + SSD_HEADS], w[:, o_i:o_i + 2 * ML_HEADS],
                               jnp.zeros((D_MODEL, pad), F32)], axis=1).astype(BF16)

    main, small = _in_proj(x2, norm_mix_w[layer].reshape(1, D_MODEL), w_main, w_small)

    zeros = jnp.zeros((SMALL_WIDTH - F_COL - ML_HEADS,), F32)
    bias_row = jnp.concatenate([dt_bias[layer], i_bias[layer], f_bias[layer], zeros]).reshape(1, SMALL_WIDTH)
    alog_row = jnp.concatenate([a_log[layer], jnp.zeros((SMALL_WIDTH - SSD_HEADS,), F32)]).reshape(1, SMALL_WIDTH)
    dskip_row = jnp.repeat(d_skip[layer].astype(F32), SSD_HEAD_DIM).reshape(1, SSD_INNER)

    y, hm = _mixers(
        main, small,
        conv_ssd_w[layer], conv_ssd_b[layer].reshape(1, SSD_XBC),
        conv_qk_w[layer], conv_qk_b[layer].reshape(1, 2 * ML_INNER),
        bias_row, alog_row, dskip_row,
        ssd_norm_w[layer].reshape(1, SSD_INNER), mlstm_norm_w[layer].reshape(1, ML_INNER),
        batch, seq)

    out = _merge(x2, y, hm, main,
                 w_br_ssd[layer].astype(BF16), w_br_mlstm[layer].astype(BF16), w_out[layer].astype(BF16),
                 norm_mlp_w[layer].reshape(1, D_MODEL), w_up[layer].astype(BF16), w_down[layer].astype(BF16),
                 norm_final_w.reshape(1, D_MODEL))
    return out.reshape(batch, seq, D_MODEL)
```

```python
import functools

import jax
import jax.numpy as jnp
import numpy as np
from jax import lax
from jax.experimental import pallas as pl
from jax.experimental.pallas import tpu as pltpu

F32 = jnp.float32
BF16 = jnp.bfloat16

D_MODEL = 1024
NORM_EPS = 1e-5
CONV_TAPS = 4
SSD_INNER = 2048
SSD_HEAD_DIM = 64
SSD_HEADS = 32
SSD_GROUPS = 4
SSD_STATE = 128
SSD_GROUP_WIDTH = SSD_INNER // SSD_GROUPS
SSD_XBC = SSD_INNER + 2 * SSD_GROUPS * SSD_STATE
ML_INNER = 1024
ML_HEADS = 4
ML_HEAD_DIM = 256
D_FF = 4096
IN_PROJ_WIDTH = SSD_INNER + SSD_XBC + SSD_HEADS + 4 * ML_INNER + 2 * ML_HEADS + 2 * D_MODEL

LANES = 128
BF16_ROWS = 16

COL_BLOCK = 1024
N_COL_BLOCKS = 11
MAIN_WIDTH = N_COL_BLOCKS * COL_BLOCK
Q_FIRST_BLOCK, G_FIRST_BLOCK = 5, 9
Q_ROW_OFFSET = SSD_HEADS
G_ROW_OFFSET = SSD_HEADS + 2 * ML_HEADS
SMALL_WIDTH = LANES
DT_COL, I_COL, F_COL = 0, 32, 36
DT_ROW_COPY = 64

CONV_WIDTH = SSD_XBC + 2 * ML_INNER
SCAN_CHUNK = 128
MIX_TOKENS = 256
CONV_PIECES = (112, 112, 32)
CONV_STRIP = 256
PROJ_TM = 2048
MERGE_TM = 512
HEADS_PER_DOT = 4
VMEM_LIMIT = 56 * 1024 * 1024


def _sigmoid(x):
    return 0.5 * jnp.tanh(0.5 * x) + 0.5


def _silu(x):
    h = 0.5 * x
    return h * jnp.tanh(h) + h


def _softplus(x):
    return jnp.maximum(x, 0.0) + jnp.log1p(jnp.exp(-jnp.abs(x)))


def _dot(a, b):
    return jnp.dot(a, b, preferred_element_type=F32)


def _dot_nt(a, b):
    return lax.dot_general(a, b, (((1,), (1,)), ((), ())), preferred_element_type=F32)


def _w_prep_kernel(w_ref, out_ref):
    out_ref[...] = w_ref[...].astype(BF16)


def _w_prep(w_t):
    def src_rows(j):
        off = jnp.where(j < Q_FIRST_BLOCK, 0, jnp.where(j < G_FIRST_BLOCK, Q_ROW_OFFSET, G_ROW_OFFSET))
        return (pl.multiple_of(j * COL_BLOCK + off, 8), 0)

    return pl.pallas_call(
        _w_prep_kernel,
        grid=(N_COL_BLOCKS,),
        in_specs=[pl.BlockSpec((pl.Element(COL_BLOCK), pl.Element(D_MODEL)), src_rows)],
        out_specs=pl.BlockSpec((COL_BLOCK, D_MODEL), lambda j: (j, 0)),
        out_shape=jax.ShapeDtypeStruct((MAIN_WIDTH, D_MODEL), BF16),
        compiler_params=pltpu.CompilerParams(
            dimension_semantics=("arbitrary",),
            vmem_limit_bytes=VMEM_LIMIT),
        name="w_prep",
    )(w_t)


def _in_proj_kernel(x_ref, nw_ref, w_ref, ws_ref, main_ref, small_ref, u_ref):
    j = pl.program_id(1)

    @pl.when(j == 0)
    def _():
        x = x_ref[...]
        ms = jnp.mean(x * x, axis=-1, keepdims=True)
        u = (x * lax.rsqrt(ms + NORM_EPS) * nw_ref[...]).astype(BF16)
        u_ref[...] = u
        small_ref[...] = _dot_nt(u, ws_ref[...].astype(BF16))

    main_ref[...] = _dot_nt(u_ref[...], w_ref[...]).astype(BF16)


def _in_proj(x2, norm_w, w_main, w_small):
    tokens = x2.shape[0]
    grid = (tokens // PROJ_TM, N_COL_BLOCKS)
    return pl.pallas_call(
        _in_proj_kernel,
        grid=grid,
        in_specs=[
            pl.BlockSpec((PROJ_TM, D_MODEL), lambda i, j: (i, 0)),
            pl.BlockSpec((1, D_MODEL), lambda i, j: (0, 0)),
            pl.BlockSpec((COL_BLOCK, D_MODEL), lambda i, j: (j, 0)),
            pl.BlockSpec((SMALL_WIDTH, D_MODEL), lambda i, j: (0, 0)),
        ],
        out_specs=[
            pl.BlockSpec((PROJ_TM, COL_BLOCK), lambda i, j: (i, j)),
            pl.BlockSpec((PROJ_TM, SMALL_WIDTH), lambda i, j: (i, 0)),
        ],
        out_shape=[
            jax.ShapeDtypeStruct((tokens, MAIN_WIDTH), BF16),
            jax.ShapeDtypeStruct((tokens, SMALL_WIDTH), F32),
        ],
        scratch_shapes=[pltpu.VMEM((PROJ_TM, D_MODEL), BF16)],
        compiler_params=pltpu.CompilerParams(
            dimension_semantics=("arbitrary", "arbitrary"),
            vmem_limit_bytes=VMEM_LIMIT),
        name="in_proj",
    )(x2, norm_w, w_main, w_small)


def _shift_select_matrix(p):
    h = BF16_ROWS
    sel = np.zeros((p, CONV_TAPS * (p + h)), np.float32)
    for tap in range(CONV_TAPS):
        for t in range(p):
            sel[t, tap * (p + h) + t + h - (CONV_TAPS - 1 - tap)] = 1.0
    return jnp.asarray(sel, BF16)


def _head_expand_matrix():
    e = np.zeros((2 * LANES, SSD_INNER), np.float32)
    for h in range(SSD_HEADS):
        e[h, h * SSD_HEAD_DIM:(h + 1) * SSD_HEAD_DIM] = 1.0
        e[LANES + h, h * SSD_HEAD_DIM:(h + 1) * SSD_HEAD_DIM] = 1.0
    return jnp.asarray(e, BF16)


def _mixer_kernel(n_blocks, z_ref, xs_ref, bc_ref, q_ref, k_ref, v_ref, o_ref, sm_ref,
                  cw_ssd_ref, cb_ssd_ref, cw_qk_ref, cb_qk_ref,
                  bias_ref, alog_ref, dskip_ref, ssd_nw_ref, ml_nw_ref, sel_a_ref, sel_b_ref, e2_ref, eye_ref,
                  y_ref, hm_ref,
                  hist_ref, conv0_ref, conv1_ref,
                  s_ref, c_ref, n_ref, m_ref, p_ref, pt_ref, dt_ref):
    tb, L = MIX_TOKENS, SCAN_CHUNK
    n_chunks = tb // L
    hp = HEADS_PER_DOT
    qw = hp * SSD_HEAD_DIM
    gw = SSD_GROUP_WIDTH
    step = pl.program_id(1)
    bc_off, q_off, k_off = SSD_INNER, SSD_XBC, SSD_XBC + ML_INNER

    @pl.when(step == 0)
    def _():
        hist_ref[...] = jnp.zeros_like(hist_ref)
        s_ref[...] = jnp.zeros_like(s_ref)
        c_ref[...] = jnp.zeros_like(c_ref)
        n_ref[...] = jnp.zeros_like(n_ref)
        m_ref[...] = jnp.zeros_like(m_ref)

    def conv_items(dst_ref):
        h = BF16_ROWS
        sels = {CONV_PIECES[0]: sel_a_ref, CONV_PIECES[-1]: sel_b_ref}
        sources = ((xs_ref, 0, SSD_INNER, cw_ssd_ref, cb_ssd_ref, 0),
                   (bc_ref, bc_off, SSD_XBC - SSD_INNER, cw_ssd_ref, cb_ssd_ref, SSD_INNER),
                   (q_ref, q_off, ML_INNER, cw_qk_ref, cb_qk_ref, 0),
                   (k_ref, k_off, ML_INNER, cw_qk_ref, cb_qk_ref, ML_INNER))

        def strip(src_ref, dst_off, w_ref, b_ref, w_off, c0):
            cols = slice(c0, c0 + CONV_STRIP)
            wcols = slice(w_off + c0, w_off + c0 + CONV_STRIP)
            dcols = slice(dst_off + c0, dst_off + c0 + CONV_STRIP)
            taps = [w_ref[tap:tap + 1, wcols].astype(BF16) for tap in range(CONV_TAPS)]
            r0 = 0
            for p in CONV_PIECES:
                if r0 == 0:
                    win = jnp.concatenate([hist_ref[:, dcols], src_ref[0:p, cols]], axis=0)
                else:
                    win = src_ref[r0 - h:r0 + p, cols]
                scaled = jnp.concatenate([win * taps[tap] for tap in range(CONV_TAPS)], axis=0)
                acc = _dot(sels[p][...], scaled) + b_ref[:, wcols]
                dst_ref[r0:r0 + p, dcols] = _silu(acc).astype(BF16)
                r0 += p
            hist_ref[:, dcols] = src_ref[tb - h:tb, cols]

        items = []
        for src_ref, dst_off, width, w_ref, b_ref, w_off in sources:
            for c0 in range(0, width, CONV_STRIP):
                items.append(lambda a=(src_ref, dst_off, w_ref, b_ref, w_off, c0): strip(*a))
        return items

    def scans(cv_ref, pending):
        pending = list(pending)
        n_slots = n_chunks * (SSD_GROUPS + 3)
        per_slot = -(-len(pending) // n_slots)

        def emit():
            for _ in range(per_slot):
                if pending:
                    pending.pop(0)()

        sm = sm_ref[...] + bias_ref[...]
        lane = lax.broadcasted_iota(jnp.int32, (tb, SMALL_WIDTH), 1)
        dt = _softplus(sm)
        log_f = -_softplus(-sm)
        a_row = -jnp.exp(alog_ref[...])
        is_dt = lane < I_COL
        is_i = (lane >= I_COL) & (lane < F_COL)
        is_f = (lane >= F_COL) & (lane < F_COL + ML_HEADS)
        pre = jnp.where(is_dt, dt * a_row, jnp.where(is_f, log_f, 0.0))
        rt = lax.broadcasted_iota(jnp.int32, (tb, tb), 0)
        rs = lax.broadcasted_iota(jnp.int32, (tb, tb), 1)
        tri = ((rs <= rt) & ((rt // L) == (rs // L))).astype(F32)
        cs = jnp.dot(tri, pre, preferred_element_type=F32, precision=lax.Precision.HIGHEST)
        dt_copy = pltpu.roll(dt, DT_ROW_COPY, axis=1)
        is_dt_copy = (lane >= DT_ROW_COPY) & (lane < DT_ROW_COPY + SSD_HEADS)
        table = jnp.where(is_dt | is_f, cs, jnp.where(is_i, sm, jnp.where(is_dt_copy, dt_copy, 0.0)))
        p_ref[...] = table
        dt_ref[...] = jnp.where(is_dt, dt, 0.0)
        for c in range(n_chunks):
            pt_ref[c] = table[c * L:(c + 1) * L, :].T

        causal = (lax.broadcasted_iota(jnp.int32, (L, L), 0) >= lax.broadcasted_iota(jnp.int32, (L, L), 1))
        dt_lanes = lax.broadcasted_iota(jnp.int32, (L, SMALL_WIDTH), 1) < I_COL
        lane_q = lax.broadcasted_iota(jnp.int32, (1, qw), 1)
        head_masks = [(lane_q >= a * SSD_HEAD_DIM) & (lane_q < (a + 1) * SSD_HEAD_DIM) for a in range(hp)]
        neg_inf = jnp.float32(-jnp.inf)

        def split_hi_lo(v):
            hi = v.astype(BF16)
            lo = (v - hi.astype(F32)).astype(BF16)
            return jnp.concatenate([hi, lo], axis=1)

        for c in range(n_chunks):
            r0 = c * L
            rows = slice(r0, r0 + L)
            tab = p_ref[rows, :]
            tab_end = p_ref[r0 + L - 1:r0 + L, :]
            tab_t = pt_ref[c]
            dt_blk = dt_ref[rows, :]
            ea2 = split_hi_lo(jnp.where(dt_lanes, jnp.exp(tab), 0.0))
            td2 = split_hi_lo(jnp.where(dt_lanes, jnp.exp(tab_end - tab) * dt_blk, 0.0))

            for g in range(SSD_GROUPS):
                emit()
                gcols = slice(g * gw, (g + 1) * gw)
                b_mat = cv_ref[rows, bc_off + g * SSD_STATE:bc_off + (g + 1) * SSD_STATE]
                c_mat = cv_ref[rows, bc_off + (SSD_GROUPS + g) * SSD_STATE:
                               bc_off + (SSD_GROUPS + g + 1) * SSD_STATE]
                cb = _dot_nt(c_mat, b_mat)
                ea_g = _dot(ea2, e2_ref[:, gcols])
                td_g = _dot(td2, e2_ref[:, gcols])
                x_g = cv_ref[rows, gcols]
                x_gf = x_g.astype(F32)
                s_g = s_ref[:, gcols]
                inter = _dot(c_mat, s_g.astype(BF16)) * ea_g
                intra_parts = []
                for qd in range(gw // qw):
                    x_q = x_g[:, qd * qw:(qd + 1) * qw]
                    zero_x = jnp.zeros_like(x_q)
                    lhs_parts, rhs_parts = [], []
                    for a in range(hp):
                        h = (g * gw + qd * qw) // SSD_HEAD_DIM + a
                        seg = tab[:, h:h + 1] - tab_t[h:h + 1, :]
                        dec = jnp.exp(jnp.where(causal, seg, neg_inf))
                        w_mat = cb * dec * tab_t[DT_ROW_COPY + h:DT_ROW_COPY + h + 1, :]
                        lhs_parts.append(w_mat.astype(BF16))
                        rhs_parts.append(jnp.where(head_masks[a], x_q, zero_x))
                    intra_parts.append(_dot(jnp.concatenate(lhs_parts, axis=1),
                                            jnp.concatenate(rhs_parts, axis=0)))
                y_g = jnp.concatenate(intra_parts, axis=1) + inter + dskip_ref[:, gcols] * x_gf
                yz = y_g * _silu(z_ref[rows, gcols].astype(F32))
                ms = jnp.mean(yz * yz, axis=-1, keepdims=True)
                y_ref[rows, gcols] = (yz * lax.rsqrt(ms + NORM_EPS) * ssd_nw_ref[:, gcols]).astype(BF16)
                xw = (x_gf * td_g).astype(BF16)
                b_t = _dot_nt(eye_ref[0:SSD_STATE, 0:SSD_STATE], b_mat).astype(BF16)
                s_ref[:, gcols] = s_g * ea_g[L - 1:L, :] + _dot(b_t, xw)

            emit()
            heads = range(ML_HEADS)
            hcols = [slice(h * ML_HEAD_DIM, (h + 1) * ML_HEAD_DIM) for h in heads]
            q_hs = [cv_ref[rows, q_off + h * ML_HEAD_DIM:q_off + (h + 1) * ML_HEAD_DIM]
                    * jnp.asarray(ML_HEAD_DIM ** -0.5, BF16) for h in heads]
            k_hs = [cv_ref[rows, k_off + h * ML_HEAD_DIM:k_off + (h + 1) * ML_HEAD_DIM] for h in heads]
            v_hs = [v_ref[rows, hcols[h]] for h in heads]
            c_prevs = [c_ref[h] for h in heads]
            n_prevs = [n_ref[h] for h in heads]
            qk_raw = [_dot_nt(q_hs[h], k_hs[h]) for h in heads]
            q_c_prev = [_dot(q_hs[h], c_prevs[h].astype(BF16)) for h in heads]
            k_ts = [_dot_nt(eye_ref[...], k_hs[h]) for h in heads]
            w_intra, w_inter, m_ts, wg2s, kw_ts, m_news, a_olds, a_locs = [], [], [], [], [], [], [], []
            for h in heads:
                b_col = tab[:, F_COL + h:F_COL + h + 1]
                b_row = tab_t[F_COL + h:F_COL + h + 1, :]
                li_row = tab_t[I_COL + h:I_COL + h + 1, :]
                b_end = tab_end[:, F_COL + h:F_COL + h + 1]
                m_prev = m_ref[0:1, h:h + 1]
                d_log = jnp.where(causal, b_col - b_row + li_row, neg_inf)
                inter_log = b_col + m_prev
                m_t = jnp.maximum(inter_log, jnp.max(d_log, axis=1, keepdims=True))
                w_intra.append(jnp.exp(d_log - m_t))
                w_inter.append(jnp.exp(inter_log - m_t))
                m_ts.append(m_t)
                g_row = b_end - b_row + li_row
                m_loc = jnp.max(g_row, axis=1, keepdims=True)
                wg_row = jnp.exp(g_row - m_loc)
                kw_ts.append((k_ts[h] * wg_row).astype(BF16))
                wg2s.append(split_hi_lo(wg_row))
                m_new = jnp.maximum(b_end + m_prev, m_loc)
                m_news.append(m_new)
                a_olds.append(jnp.exp(b_end + m_prev - m_new))
                a_locs.append(jnp.exp(m_loc - m_new))
            emit()
            c_locs = [_dot(kw_ts[h], v_hs[h]) for h in heads]
            n_locs = [_dot(wg2s[h], jnp.concatenate([k_hs[h], k_hs[h]], axis=0)) for h in heads]
            qks = [qk_raw[h] * w_intra[h] for h in heads]
            nums = [_dot(qks[h].astype(BF16), v_hs[h]) + w_inter[h] * q_c_prev[h] for h in heads]
            emit()
            for h in heads:
                den = (jnp.sum(qks[h], axis=1, keepdims=True)
                       + w_inter[h] * jnp.sum(q_hs[h].astype(F32) * n_prevs[h], axis=1, keepdims=True))
                hh = nums[h] / jnp.maximum(jnp.abs(den), jnp.exp(-m_ts[h]))
                ms = jnp.mean(hh * hh, axis=-1, keepdims=True)
                hn = hh * lax.rsqrt(ms + NORM_EPS) * ml_nw_ref[:, hcols[h]]
                hm_ref[rows, hcols[h]] = (_sigmoid(o_ref[rows, hcols[h]].astype(F32)) * hn).astype(BF16)
            for h in heads:
                c_ref[h] = a_olds[h] * c_prevs[h] + a_locs[h] * c_locs[h]
                n_ref[h] = a_olds[h] * n_prevs[h] + a_locs[h] * n_locs[h]
                m_ref[0:1, h:h + 1] = m_news[h]
        while pending:
            pending.pop(0)()

    bufs = (conv0_ref, conv1_ref)

    @pl.when(step == 0)
    def _():
        for item in conv_items(bufs[0]):
            item()

    for parity in range(2):
        @pl.when((step > 0) & (step < n_blocks) & (step % 2 == parity))
        def _():
            scans(bufs[1 - parity], conv_items(bufs[parity]))

    @pl.when(step == n_blocks)
    def _():
        scans(bufs[(n_blocks - 1) % 2], [])


def _mixers(main, small, cw_ssd, cb_ssd, cw_qk, cb_qk,
            bias_row, alog_row, dskip_row, ssd_nw, ml_nw, batch, seq):
    tb, L = MIX_TOKENS, SCAN_CHUNK
    spb = seq // tb
    tokens = batch * seq
    assert sum(CONV_PIECES) == tb and len(set(CONV_PIECES)) == 2
    sel_a = _shift_select_matrix(CONV_PIECES[0])
    sel_b = _shift_select_matrix(CONV_PIECES[-1])
    e2 = _head_expand_matrix()
    eye = jnp.eye(ML_HEAD_DIM, dtype=BF16)

    def ahead(width, idx):
        return pl.BlockSpec((tb, width), lambda b, s: (b * spb + jnp.minimum(s, spb - 1), idx))

    def behind(width, idx):
        return pl.BlockSpec((tb, width), lambda b, s: (b * spb + jnp.maximum(s - 1, 0), idx))

    def const(shape):
        return pl.BlockSpec(shape, lambda b, s: tuple(0 for _ in shape))

    in_specs = [
        behind(2 * COL_BLOCK, 0),
        ahead(2 * COL_BLOCK, 1),
        ahead(COL_BLOCK, 4),
        ahead(COL_BLOCK, 5),
        ahead(COL_BLOCK, 6),
        behind(COL_BLOCK, 7),
        behind(COL_BLOCK, 8),
        behind(SMALL_WIDTH, 0),
        const((CONV_TAPS, SSD_XBC)), const((1, SSD_XBC)),
        const((CONV_TAPS, 2 * ML_INNER)), const((1, 2 * ML_INNER)),
        const((1, SMALL_WIDTH)), const((1, SMALL_WIDTH)),
        const((1, SSD_INNER)), const((1, SSD_INNER)), const((1, ML_INNER)),
        const(sel_a.shape), const(sel_b.shape), const(e2.shape), const(eye.shape),
    ]
    out_specs = [behind(SSD_INNER, 0), behind(ML_INNER, 0)]
    scratch = [
        pltpu.VMEM((BF16_ROWS, CONV_WIDTH), BF16),
        pltpu.VMEM((tb, CONV_WIDTH), BF16),
        pltpu.VMEM((tb, CONV_WIDTH), BF16),
        pltpu.VMEM((SSD_STATE, SSD_INNER), F32),
        pltpu.VMEM((ML_HEADS, ML_HEAD_DIM, ML_HEAD_DIM), F32),
        pltpu.VMEM((ML_HEADS, 1, ML_HEAD_DIM), F32),
        pltpu.VMEM((8, LANES), F32),
        pltpu.VMEM((tb, SMALL_WIDTH), F32),
        pltpu.VMEM((tb // L, SMALL_WIDTH, L), F32),
        pltpu.VMEM((tb, SMALL_WIDTH), F32),
    ]
    return pl.pallas_call(
        functools.partial(_mixer_kernel, spb),
        grid=(batch, spb + 1),
        in_specs=in_specs,
        out_specs=out_specs,
        out_shape=[jax.ShapeDtypeStruct((tokens, SSD_INNER), BF16),
                   jax.ShapeDtypeStruct((tokens, ML_INNER), BF16)],
        scratch_shapes=scratch,
        compiler_params=pltpu.CompilerParams(
            dimension_semantics=("arbitrary", "arbitrary"),
            vmem_limit_bytes=VMEM_LIMIT),
        name="mixers",
    )(main, main, main, main, main, main, main, small,
      cw_ssd, cb_ssd, cw_qk, cb_qk,
      bias_row, alog_row, dskip_row, ssd_nw, ml_nw, sel_a, sel_b, e2, eye)


def _rms(x, w):
    ms = jnp.mean(x * x, axis=-1, keepdims=True)
    return x * lax.rsqrt(ms + NORM_EPS) * w


def _merge_kernel(x_ref, y_ref, hm_ref, gs_ref, gm_ref,
                  wbs_ref, wbm_ref, wo_ref, nmw_ref, wup_ref, wdn_ref, nfw_ref, out_ref):
    a = _dot(y_ref[...], wbs_ref[...])
    b = _dot(hm_ref[...], wbm_ref[...])
    mixed = _sigmoid(gs_ref[...].astype(F32)) * a + _sigmoid(gm_ref[...].astype(F32)) * b
    h1 = x_ref[...] + _dot(mixed.astype(BF16), wo_ref[...])
    u = _rms(h1, nmw_ref[...]).astype(BF16)
    up = jnp.maximum(_dot(u, wup_ref[...]), 0.0)
    act = (up * up).astype(BF16)
    h2 = h1 + _dot(act, wdn_ref[...])
    out_ref[...] = _rms(h2, nfw_ref[...])


def _merge(x2, y, hm, main, w_br_ssd, w_br_ml, w_out, norm_mlp_w, w_up, w_down, norm_final_w):
    tokens = x2.shape[0]
    tm = MERGE_TM

    def resident(shape):
        return pl.BlockSpec(shape, lambda i: (0, 0), pipeline_mode=pl.Buffered(1))

    return pl.pallas_call(
        _merge_kernel,
        grid=(tokens // tm,),
        in_specs=[
            pl.BlockSpec((tm, D_MODEL), lambda i: (i, 0)),
            pl.BlockSpec((tm, SSD_INNER), lambda i: (i, 0)),
            pl.BlockSpec((tm, ML_INNER), lambda i: (i, 0)),
            pl.BlockSpec((tm, COL_BLOCK), lambda i: (i, 9)),
            pl.BlockSpec((tm, COL_BLOCK), lambda i: (i, 10)),
            resident((SSD_INNER, D_MODEL)),
            resident((ML_INNER, D_MODEL)),
            resident((D_MODEL, D_MODEL)),
            resident((1, D_MODEL)),
            resident((D_MODEL, D_FF)),
            resident((D_FF, D_MODEL)),
            resident((1, D_MODEL)),
        ],
        out_specs=pl.BlockSpec((tm, D_MODEL), lambda i: (i, 0)),
        out_shape=jax.ShapeDtypeStruct((tokens, D_MODEL), F32),
        compiler_params=pltpu.CompilerParams(
            dimension_semantics=("arbitrary",),
            vmem_limit_bytes=VMEM_LIMIT),
        name="merge_mlp",
    )(x2, y, hm, main, main, w_br_ssd, w_br_ml, w_out, norm_mlp_w, w_up, w_down, norm_final_w)


def kernel(x, norm_mix_w, w_in, conv_ssd_w, conv_ssd_b, dt_bias, a_log, d_skip, ssd_norm_w,
           conv_qk_w, conv_qk_b, i_bias, f_bias, mlstm_norm_w, w_br_ssd, w_br_mlstm, w_out,
           norm_mlp_w, w_up, w_down, norm_final_w):
    batch, seq, _ = x.shape
    x2 = x.reshape(batch * seq, D_MODEL)
    layer = 0

    w_t = jnp.transpose(w_in[layer])
    w_main = _w_prep(w_t)
    o_dt = SSD_INNER + SSD_XBC
    o_i = o_dt + SSD_HEADS + 4 * ML_INNER
    pad = SMALL_WIDTH - SSD_HEADS - 2 * ML_HEADS
    w_small = jnp.concatenate([w_t[o_dt:o_dt + SSD_HEADS], w_t[o_i:o_i + 2 * ML_HEADS],
                               jnp.zeros((pad, D_MODEL), F32)], axis=0)

    main, small = _in_proj(x2, norm_mix_w[layer].reshape(1, D_MODEL), w_main, w_small)

    zeros = jnp.zeros((SMALL_WIDTH - F_COL - ML_HEADS,), F32)
    bias_row = jnp.concatenate([dt_bias[layer], i_bias[layer], f_bias[layer], zeros]).reshape(1, SMALL_WIDTH)
    alog_row = jnp.concatenate([a_log[layer], jnp.zeros((SMALL_WIDTH - SSD_HEADS,), F32)]).reshape(1, SMALL_WIDTH)
    dskip_row = jnp.repeat(d_skip[layer].astype(F32), SSD_HEAD_DIM).reshape(1, SSD_INNER)

    y, hm = _mixers(
        main, small,
        conv_ssd_w[layer], conv_ssd_b[layer].reshape(1, SSD_XBC),
        conv_qk_w[layer], conv_qk_b[layer].reshape(1, 2 * ML_INNER),
        bias_row, alog_row, dskip_row,
        ssd_norm_w[layer].reshape(1, SSD_INNER), mlstm_norm_w[layer].reshape(1, ML_INNER),
        batch, seq)

    out = _merge(x2, y, hm, main,
                 w_br_ssd[layer].astype(BF16), w_br_mlstm[layer].astype(BF16), w_out[layer].astype(BF16),
                 norm_mlp_w[layer].reshape(1, D_MODEL), w_up[layer].astype(BF16), w_down[layer].astype(BF16),
                 norm_final_w.reshape(1, D_MODEL))
    return out.reshape(batch, seq, D_MODEL)
```

```python
import functools

import jax
import jax.numpy as jnp
import numpy as np
from jax import lax
from jax.experimental import pallas as pl
from jax.experimental.pallas import tpu as pltpu

F32 = jnp.float32
BF16 = jnp.bfloat16

LOG2E = 1.4426950408889634

D_MODEL = 1024
NORM_EPS = 1e-5
CONV_TAPS = 4
SSD_INNER = 2048
SSD_HEAD_DIM = 64
SSD_HEADS = 32
SSD_GROUPS = 4
SSD_STATE = 128
SSD_GROUP_WIDTH = SSD_INNER // SSD_GROUPS
SSD_XBC = SSD_INNER + 2 * SSD_GROUPS * SSD_STATE
ML_INNER = 1024
ML_HEADS = 4
ML_HEAD_DIM = 256
D_FF = 4096
IN_PROJ_WIDTH = SSD_INNER + SSD_XBC + SSD_HEADS + 4 * ML_INNER + 2 * ML_HEADS + 2 * D_MODEL

LANES = 128
BF16_ROWS = 16

COL_BLOCK = 1024
N_COL_BLOCKS = 11
MAIN_WIDTH = N_COL_BLOCKS * COL_BLOCK
Q_FIRST_BLOCK, G_FIRST_BLOCK = 5, 9
Q_ROW_OFFSET = SSD_HEADS
G_ROW_OFFSET = SSD_HEADS + 2 * ML_HEADS
SMALL_WIDTH = LANES
DT_COL, I_COL, F_COL = 0, 32, 36
DT_ROW_COPY = 64

CONV_WIDTH = SSD_XBC + 2 * ML_INNER
SCAN_CHUNK = 128
MIX_TOKENS = 256
CONV_PIECES = (112, 112, 32)
CONV_STRIP = 256
PROJ_TM = 2048
MERGE_TM = 512
HEADS_PER_DOT = 4
VMEM_LIMIT = 56 * 1024 * 1024


def _sigmoid(x):
    return 0.5 * jnp.tanh(0.5 * x) + 0.5


def _silu(x):
    h = 0.5 * x
    return h * jnp.tanh(h) + h


def _softplus(x):
    return jnp.maximum(x, 0.0) + jnp.log1p(jnp.exp(-jnp.abs(x)))


def _dot(a, b):
    return jnp.dot(a, b, preferred_element_type=F32)


def _dot_nt(a, b):
    return lax.dot_general(a, b, (((1,), (1,)), ((), ())), preferred_element_type=F32)


def _w_prep_kernel(w_ref, out_ref):
    out_ref[...] = w_ref[...].astype(BF16)


def _w_prep(w_t):
    def src_rows(j):
        off = jnp.where(j < Q_FIRST_BLOCK, 0, jnp.where(j < G_FIRST_BLOCK, Q_ROW_OFFSET, G_ROW_OFFSET))
        return (pl.multiple_of(j * COL_BLOCK + off, 8), 0)

    return pl.pallas_call(
        _w_prep_kernel,
        grid=(N_COL_BLOCKS,),
        in_specs=[pl.BlockSpec((pl.Element(COL_BLOCK), pl.Element(D_MODEL)), src_rows)],
        out_specs=pl.BlockSpec((COL_BLOCK, D_MODEL), lambda j: (j, 0)),
        out_shape=jax.ShapeDtypeStruct((MAIN_WIDTH, D_MODEL), BF16),
        compiler_params=pltpu.CompilerParams(
            dimension_semantics=("arbitrary",),
            vmem_limit_bytes=VMEM_LIMIT),
        name="w_prep",
    )(w_t)


def _in_proj_kernel(x_ref, nw_ref, w_ref, ws_ref, main_ref, small_ref, u_ref):
    j = pl.program_id(1)

    @pl.when(j == 0)
    def _():
        x = x_ref[...]
        ms = jnp.mean(x * x, axis=-1, keepdims=True)
        u = (x * lax.rsqrt(ms + NORM_EPS) * nw_ref[...]).astype(BF16)
        u_ref[...] = u
        small_ref[...] = _dot_nt(u, ws_ref[...].astype(BF16))

    main_ref[...] = _dot_nt(u_ref[...], w_ref[...]).astype(BF16)


def _in_proj(x2, norm_w, w_main, w_small):
    tokens = x2.shape[0]
    grid = (tokens // PROJ_TM, N_COL_BLOCKS)
    return pl.pallas_call(
        _in_proj_kernel,
        grid=grid,
        in_specs=[
            pl.BlockSpec((PROJ_TM, D_MODEL), lambda i, j: (i, 0)),
            pl.BlockSpec((1, D_MODEL), lambda i, j: (0, 0)),
            pl.BlockSpec((COL_BLOCK, D_MODEL), lambda i, j: (j, 0)),
            pl.BlockSpec((SMALL_WIDTH, D_MODEL), lambda i, j: (0, 0)),
        ],
        out_specs=[
            pl.BlockSpec((PROJ_TM, COL_BLOCK), lambda i, j: (i, j)),
            pl.BlockSpec((PROJ_TM, SMALL_WIDTH), lambda i, j: (i, 0)),
        ],
        out_shape=[
            jax.ShapeDtypeStruct((tokens, MAIN_WIDTH), BF16),
            jax.ShapeDtypeStruct((tokens, SMALL_WIDTH), F32),
        ],
        scratch_shapes=[pltpu.VMEM((PROJ_TM, D_MODEL), BF16)],
        compiler_params=pltpu.CompilerParams(
            dimension_semantics=("arbitrary", "arbitrary"),
            vmem_limit_bytes=VMEM_LIMIT),
        name="in_proj",
    )(x2, norm_w, w_main, w_small)


def _shift_select_matrix(p):
    h = BF16_ROWS
    sel = np.zeros((p, CONV_TAPS * (p + h)), np.float32)
    for tap in range(CONV_TAPS):
        for t in range(p):
            sel[t, tap * (p + h) + t + h - (CONV_TAPS - 1 - tap)] = 1.0
    return jnp.asarray(sel, BF16)


def _head_expand_matrix():
    e = np.zeros((2 * LANES, SSD_INNER), np.float32)
    for h in range(SSD_HEADS):
        e[h, h * SSD_HEAD_DIM:(h + 1) * SSD_HEAD_DIM] = 1.0
        e[LANES + h, h * SSD_HEAD_DIM:(h + 1) * SSD_HEAD_DIM] = 1.0
    return jnp.asarray(e, BF16)


def _mixer_kernel(n_blocks, z_ref, xs_ref, bc_ref, q_ref, k_ref, v_ref, o_ref, sm_ref,
                  cw_ssd_ref, cb_ssd_ref, cw_qk_ref, cb_qk_ref,
                  bias_ref, alog_ref, dskip_ref, ssd_nw_ref, ml_nw_ref, sel_a_ref, sel_b_ref, e2_ref, eye_ref,
                  y_ref, hm_ref,
                  hist_ref, conv0_ref, conv1_ref,
                  s_ref, c_ref, n_ref, m_ref, p_ref, pt_ref, dt_ref):
    tb, L = MIX_TOKENS, SCAN_CHUNK
    n_chunks = tb // L
    hp = HEADS_PER_DOT
    qw = hp * SSD_HEAD_DIM
    gw = SSD_GROUP_WIDTH
    step = pl.program_id(1)
    bc_off, q_off, k_off = SSD_INNER, SSD_XBC, SSD_XBC + ML_INNER

    @pl.when(step == 0)
    def _():
        hist_ref[...] = jnp.zeros_like(hist_ref)
        s_ref[...] = jnp.zeros_like(s_ref)
        c_ref[...] = jnp.zeros_like(c_ref)
        n_ref[...] = jnp.zeros_like(n_ref)
        m_ref[...] = jnp.zeros_like(m_ref)

    def conv_items(dst_ref):
        h = BF16_ROWS
        sels = {CONV_PIECES[0]: sel_a_ref, CONV_PIECES[-1]: sel_b_ref}
        sources = ((xs_ref, 0, SSD_INNER, cw_ssd_ref, cb_ssd_ref, 0),
                   (bc_ref, bc_off, SSD_XBC - SSD_INNER, cw_ssd_ref, cb_ssd_ref, SSD_INNER),
                   (q_ref, q_off, ML_INNER, cw_qk_ref, cb_qk_ref, 0),
                   (k_ref, k_off, ML_INNER, cw_qk_ref, cb_qk_ref, ML_INNER))

        def strip(src_ref, dst_off, w_ref, b_ref, w_off, c0):
            cols = slice(c0, c0 + CONV_STRIP)
            wcols = slice(w_off + c0, w_off + c0 + CONV_STRIP)
            dcols = slice(dst_off + c0, dst_off + c0 + CONV_STRIP)
            taps = [jnp.broadcast_to(0.5 * w_ref[tap:tap + 1, wcols], (h, CONV_STRIP)).astype(BF16)
                    for tap in range(CONV_TAPS)]
            half_bias = 0.5 * b_ref[:, wcols]
            r0 = 0
            for p in CONV_PIECES:
                if r0 == 0:
                    win = jnp.concatenate([hist_ref[:, dcols], src_ref[0:p, cols]], axis=0)
                else:
                    win = src_ref[r0 - h:r0 + p, cols]
                win = win.reshape((p + h) // h, h, CONV_STRIP)
                scaled = jnp.concatenate([(win * taps[tap][None]).reshape(p + h, CONV_STRIP)
                                          for tap in range(CONV_TAPS)], axis=0)
                g = _dot(sels[p][...], scaled) + half_bias
                dst_ref[r0:r0 + p, dcols] = (g * jnp.tanh(g) + g).astype(BF16)
                r0 += p
            hist_ref[:, dcols] = src_ref[tb - h:tb, cols]

        items = []
        for src_ref, dst_off, width, w_ref, b_ref, w_off in sources:
            for c0 in range(0, width, CONV_STRIP):
                items.append(lambda a=(src_ref, dst_off, w_ref, b_ref, w_off, c0): strip(*a))
        return items

    def scans(cv_ref, pending):
        pending = list(pending)
        n_slots = n_chunks * (SSD_GROUPS + 3)
        per_slot = -(-len(pending) // n_slots)

        def emit():
            for _ in range(per_slot):
                if pending:
                    pending.pop(0)()

        sm = sm_ref[...] + bias_ref[...]
        lane = lax.broadcasted_iota(jnp.int32, (tb, SMALL_WIDTH), 1)
        dt = _softplus(sm)
        log_f = -_softplus(-sm)
        a_row = -jnp.exp(alog_ref[...])
        is_dt = lane < I_COL
        is_i = (lane >= I_COL) & (lane < F_COL)
        is_f = (lane >= F_COL) & (lane < F_COL + ML_HEADS)
        pre = jnp.where(is_dt, dt * a_row, jnp.where(is_f, log_f, 0.0))
        rt = lax.broadcasted_iota(jnp.int32, (tb, tb), 0)
        rs = lax.broadcasted_iota(jnp.int32, (tb, tb), 1)
        tri = ((rs <= rt) & ((rt // L) == (rs // L))).astype(F32)
        cs = jnp.dot(tri, pre, preferred_element_type=F32, precision=lax.Precision.HIGHEST)
        dt_copy = pltpu.roll(dt, DT_ROW_COPY, axis=1)
        is_dt_copy = (lane >= DT_ROW_COPY) & (lane < DT_ROW_COPY + SSD_HEADS)
        table = jnp.where(is_dt | is_f, cs * LOG2E,
                          jnp.where(is_i, sm * LOG2E, jnp.where(is_dt_copy, dt_copy, 0.0)))
        p_ref[...] = table
        dt_ref[...] = jnp.where(is_dt, dt, 0.0)
        for c in range(n_chunks):
            pt_ref[c] = table[c * L:(c + 1) * L, :].T

        causal = (lax.broadcasted_iota(jnp.int32, (L, L), 0) >= lax.broadcasted_iota(jnp.int32, (L, L), 1))
        dt_lanes = lax.broadcasted_iota(jnp.int32, (L, SMALL_WIDTH), 1) < I_COL
        lane_q = lax.broadcasted_iota(jnp.int32, (1, qw), 1)
        head_masks = [(lane_q >= a * SSD_HEAD_DIM) & (lane_q < (a + 1) * SSD_HEAD_DIM) for a in range(hp)]
        neg_inf = jnp.float32(-jnp.inf)

        def split_hi_lo(v):
            hi = v.astype(BF16)
            lo = (v - hi.astype(F32)).astype(BF16)
            return jnp.concatenate([hi, lo], axis=1)

        for c in range(n_chunks):
            r0 = c * L
            rows = slice(r0, r0 + L)
            tab = p_ref[rows, :]
            tab_end = p_ref[r0 + L - 1:r0 + L, :]
            tab_t = pt_ref[c]
            dt_blk = dt_ref[rows, :]
            ea2 = split_hi_lo(jnp.where(dt_lanes, jnp.exp2(tab), 0.0))
            td2 = split_hi_lo(jnp.where(dt_lanes, jnp.exp2(tab_end - tab) * dt_blk, 0.0))

            for g in range(SSD_GROUPS):
                emit()
                gcols = slice(g * gw, (g + 1) * gw)
                b_mat = cv_ref[rows, bc_off + g * SSD_STATE:bc_off + (g + 1) * SSD_STATE]
                c_mat = cv_ref[rows, bc_off + (SSD_GROUPS + g) * SSD_STATE:
                               bc_off + (SSD_GROUPS + g + 1) * SSD_STATE]
                cb = _dot_nt(c_mat, b_mat)
                ea_g = _dot(ea2, e2_ref[:, gcols])
                td_g = _dot(td2, e2_ref[:, gcols])
                x_g = cv_ref[rows, gcols]
                x_gf = x_g.astype(F32)
                s_g = s_ref[:, gcols]
                inter = _dot(c_mat, s_g.astype(BF16)) * ea_g
                intra_parts = []
                for qd in range(gw // qw):
                    x_q = x_g[:, qd * qw:(qd + 1) * qw]
                    zero_x = jnp.zeros_like(x_q)
                    lhs_parts, rhs_parts = [], []
                    for a in range(hp):
                        h = (g * gw + qd * qw) // SSD_HEAD_DIM + a
                        seg = tab[:, h:h + 1] - tab_t[h:h + 1, :]
                        dec = jnp.exp2(jnp.where(causal, seg, neg_inf))
                        w_mat = cb * dec * tab_t[DT_ROW_COPY + h:DT_ROW_COPY + h + 1, :]
                        lhs_parts.append(w_mat.astype(BF16))
                        rhs_parts.append(jnp.where(head_masks[a], x_q, zero_x))
                    intra_parts.append(_dot(jnp.concatenate(lhs_parts, axis=1),
                                            jnp.concatenate(rhs_parts, axis=0)))
                y_g = jnp.concatenate(intra_parts, axis=1) + inter + dskip_ref[:, gcols] * x_gf
                yz = y_g * _silu(z_ref[rows, gcols].astype(F32))
                ms = jnp.mean(yz * yz, axis=-1, keepdims=True)
                y_ref[rows, gcols] = (yz * lax.rsqrt(ms + NORM_EPS) * ssd_nw_ref[:, gcols]).astype(BF16)
                xw = (x_gf * td_g).astype(BF16)
                b_t = _dot_nt(eye_ref[0:SSD_STATE, 0:SSD_STATE], b_mat).astype(BF16)
                s_ref[:, gcols] = s_g * ea_g[L - 1:L, :] + _dot(b_t, xw)

            emit()
            heads = range(ML_HEADS)
            hcols = [slice(h * ML_HEAD_DIM, (h + 1) * ML_HEAD_DIM) for h in heads]
            q_hs = [cv_ref[rows, q_off + h * ML_HEAD_DIM:q_off + (h + 1) * ML_HEAD_DIM]
                    * jnp.asarray(ML_HEAD_DIM ** -0.5, BF16) for h in heads]
            k_hs = [cv_ref[rows, k_off + h * ML_HEAD_DIM:k_off + (h + 1) * ML_HEAD_DIM] for h in heads]
            v_hs = [v_ref[rows, hcols[h]] for h in heads]
            c_prevs = [c_ref[h] for h in heads]
            n_prevs = [n_ref[h] for h in heads]
            qk_raw = [_dot_nt(q_hs[h], k_hs[h]) for h in heads]
            q_c_prev = [_dot(q_hs[h], c_prevs[h].astype(BF16)) for h in heads]
            k_ts = [_dot_nt(eye_ref[...], k_hs[h]) for h in heads]
            w_intra, w_inter, m_ts, wg2s, kw_ts, m_news, a_olds, a_locs = [], [], [], [], [], [], [], []
            for h in heads:
                b_col = tab[:, F_COL + h:F_COL + h + 1]
                b_row = tab_t[F_COL + h:F_COL + h + 1, :]
                li_row = tab_t[I_COL + h:I_COL + h + 1, :]
                b_end = tab_end[:, F_COL + h:F_COL + h + 1]
                m_prev = m_ref[0:1, h:h + 1]
                d_log = jnp.where(causal, b_col - b_row + li_row, neg_inf)
                inter_log = b_col + m_prev
                m_t = jnp.maximum(inter_log, jnp.max(d_log, axis=1, keepdims=True))
                w_intra.append(jnp.exp2(d_log - m_t))
                w_inter.append(jnp.exp2(inter_log - m_t))
                m_ts.append(m_t)
                g_row = b_end - b_row + li_row
                m_loc = jnp.max(g_row, axis=1, keepdims=True)
                wg_row = jnp.exp2(g_row - m_loc)
                kw_ts.append((k_ts[h] * wg_row).astype(BF16))
                wg2s.append(split_hi_lo(wg_row))
                m_new = jnp.maximum(b_end + m_prev, m_loc)
                m_news.append(m_new)
                a_olds.append(jnp.exp2(b_end + m_prev - m_new))
                a_locs.append(jnp.exp2(m_loc - m_new))
            emit()
            c_locs = [_dot(kw_ts[h], v_hs[h]) for h in heads]
            n_locs = [_dot(wg2s[h], jnp.concatenate([k_hs[h], k_hs[h]], axis=0)) for h in heads]
            qks = [qk_raw[h] * w_intra[h] for h in heads]
            nums = [_dot(qks[h].astype(BF16), v_hs[h]) + w_inter[h] * q_c_prev[h] for h in heads]
            emit()
            for h in heads:
                den = (jnp.sum(qks[h], axis=1, keepdims=True)
                       + w_inter[h] * jnp.sum(q_hs[h].astype(F32) * n_prevs[h], axis=1, keepdims=True))
                hh = nums[h] / jnp.maximum(jnp.abs(den), jnp.exp2(-m_ts[h]))
                ms = jnp.mean(hh * hh, axis=-1, keepdims=True)
                hn = hh * lax.rsqrt(ms + NORM_EPS) * ml_nw_ref[:, hcols[h]]
                hm_ref[rows, hcols[h]] = (_sigmoid(o_ref[rows, hcols[h]].astype(F32)) * hn).astype(BF16)
            for h in heads:
                c_ref[h] = a_olds[h] * c_prevs[h] + a_locs[h] * c_locs[h]
                n_ref[h] = a_olds[h] * n_prevs[h] + a_locs[h] * n_locs[h]
                m_ref[0:1, h:h + 1] = m_news[h]
        while pending:
            pending.pop(0)()

    bufs = (conv0_ref, conv1_ref)

    @pl.when(step == 0)
    def _():
        for item in conv_items(bufs[0]):
            item()

    for parity in range(2):
        @pl.when((step > 0) & (step < n_blocks) & (step % 2 == parity))
        def _():
            scans(bufs[1 - parity], conv_items(bufs[parity]))

    @pl.when(step == n_blocks)
    def _():
        scans(bufs[(n_blocks - 1) % 2], [])


def _mixers(main, small, cw_ssd, cb_ssd, cw_qk, cb_qk,
            bias_row, alog_row, dskip_row, ssd_nw, ml_nw, batch, seq):
    tb, L = MIX_TOKENS, SCAN_CHUNK
    spb = seq // tb
    tokens = batch * seq
    assert sum(CONV_PIECES) == tb and len(set(CONV_PIECES)) == 2
    sel_a = _shift_select_matrix(CONV_PIECES[0])
    sel_b = _shift_select_matrix(CONV_PIECES[-1])
    e2 = _head_expand_matrix()
    eye = jnp.eye(ML_HEAD_DIM, dtype=BF16)

    def ahead(width, idx):
        return pl.BlockSpec((tb, width), lambda b, s: (b * spb + jnp.minimum(s, spb - 1), idx))

    def behind(width, idx):
        return pl.BlockSpec((tb, width), lambda b, s: (b * spb + jnp.maximum(s - 1, 0), idx))

    def const(shape):
        return pl.BlockSpec(shape, lambda b, s: tuple(0 for _ in shape))

    in_specs = [
        behind(2 * COL_BLOCK, 0),
        ahead(2 * COL_BLOCK, 1),
        ahead(COL_BLOCK, 4),
        ahead(COL_BLOCK, 5),
        ahead(COL_BLOCK, 6),
        behind(COL_BLOCK, 7),
        behind(COL_BLOCK, 8),
        behind(SMALL_WIDTH, 0),
        const((CONV_TAPS, SSD_XBC)), const((1, SSD_XBC)),
        const((CONV_TAPS, 2 * ML_INNER)), const((1, 2 * ML_INNER)),
        const((1, SMALL_WIDTH)), const((1, SMALL_WIDTH)),
        const((1, SSD_INNER)), const((1, SSD_INNER)), const((1, ML_INNER)),
        const(sel_a.shape), const(sel_b.shape), const(e2.shape), const(eye.shape),
    ]
    out_specs = [behind(SSD_INNER, 0), behind(ML_INNER, 0)]
    scratch = [
        pltpu.VMEM((BF16_ROWS, CONV_WIDTH), BF16),
        pltpu.VMEM((tb, CONV_WIDTH), BF16),
        pltpu.VMEM((tb, CONV_WIDTH), BF16),
        pltpu.VMEM((SSD_STATE, SSD_INNER), F32),
        pltpu.VMEM((ML_HEADS, ML_HEAD_DIM, ML_HEAD_DIM), F32),
        pltpu.VMEM((ML_HEADS, 1, ML_HEAD_DIM), F32),
        pltpu.VMEM((8, LANES), F32),
        pltpu.VMEM((tb, SMALL_WIDTH), F32),
        pltpu.VMEM((tb // L, SMALL_WIDTH, L), F32),
        pltpu.VMEM((tb, SMALL_WIDTH), F32),
    ]
    return pl.pallas_call(
        functools.partial(_mixer_kernel, spb),
        grid=(batch, spb + 1),
        in_specs=in_specs,
        out_specs=out_specs,
        out_shape=[jax.ShapeDtypeStruct((tokens, SSD_INNER), BF16),
                   jax.ShapeDtypeStruct((tokens, ML_INNER), BF16)],
        scratch_shapes=scratch,
        compiler_params=pltpu.CompilerParams(
            dimension_semantics=("arbitrary", "arbitrary"),
            vmem_limit_bytes=VMEM_LIMIT),
        name="mixers",
    )(main, main, main, main, main, main, main, small,
      cw_ssd, cb_ssd, cw_qk, cb_qk,
      bias_row, alog_row, dskip_row, ssd_nw, ml_nw, sel_a, sel_b, e2, eye)


def _rms(x, w):
    ms = jnp.mean(x * x, axis=-1, keepdims=True)
    return x * lax.rsqrt(ms + NORM_EPS) * w


def _merge_kernel(x_ref, y_ref, hm_ref, gs_ref, gm_ref,
                  wbs_ref, wbm_ref, wo_ref, nmw_ref, wup_ref, wdn_ref, nfw_ref, out_ref):
    a = _dot(y_ref[...], wbs_ref[...])
    b = _dot(hm_ref[...], wbm_ref[...])
    mixed = _sigmoid(gs_ref[...].astype(F32)) * a + _sigmoid(gm_ref[...].astype(F32)) * b
    h1 = x_ref[...] + _dot(mixed.astype(BF16), wo_ref[...])
    u = _rms(h1, nmw_ref[...]).astype(BF16)
    up = jnp.maximum(_dot(u, wup_ref[...]), 0.0)
    act = (up * up).astype(BF16)
    h2 = h1 + _dot(act, wdn_ref[...])
    out_ref[...] = _rms(h2, nfw_ref[...])


def _merge(x2, y, hm, main, w_br_ssd, w_br_ml, w_out, norm_mlp_w, w_up, w_down, norm_final_w):
    tokens = x2.shape[0]
    tm = MERGE_TM

    def resident(shape):
        return pl.BlockSpec(shape, lambda i: (0, 0), pipeline_mode=pl.Buffered(1))

    return pl.pallas_call(
        _merge_kernel,
        grid=(tokens // tm,),
        in_specs=[
            pl.BlockSpec((tm, D_MODEL), lambda i: (i, 0)),
            pl.BlockSpec((tm, SSD_INNER), lambda i: (i, 0)),
            pl.BlockSpec((tm, ML_INNER), lambda i: (i, 0)),
            pl.BlockSpec((tm, COL_BLOCK), lambda i: (i, 9)),
            pl.BlockSpec((tm, COL_BLOCK), lambda i: (i, 10)),
            resident((SSD_INNER, D_MODEL)),
            resident((ML_INNER, D_MODEL)),
            resident((D_MODEL, D_MODEL)),
            resident((1, D_MODEL)),
            resident((D_MODEL, D_FF)),
            resident((D_FF, D_MODEL)),
            resident((1, D_MODEL)),
        ],
        out_specs=pl.BlockSpec((tm, D_MODEL), lambda i: (i, 0)),
        out_shape=jax.ShapeDtypeStruct((tokens, D_MODEL), F32),
        compiler_params=pltpu.CompilerParams(
            dimension_semantics=("arbitrary",),
            vmem_limit_bytes=VMEM_LIMIT),
        name="merge_mlp",
    )(x2, y, hm, main, main, w_br_ssd, w_br_ml, w_out, norm_mlp_w, w_up, w_down, norm_final_w)


def kernel(x, norm_mix_w, w_in, conv_ssd_w, conv_ssd_b, dt_bias, a_log, d_skip, ssd_norm_w,
           conv_qk_w, conv_qk_b, i_bias, f_bias, mlstm_norm_w, w_br_ssd, w_br_mlstm, w_out,
           norm_mlp_w, w_up, w_down, norm_final_w):
    batch, seq, _ = x.shape
    x2 = x.reshape(batch * seq, D_MODEL)
    layer = 0

    w_t = jnp.transpose(w_in[layer])
    w_main = _w_prep(w_t)
    o_dt = SSD_INNER + SSD_XBC
    o_i = o_dt + SSD_HEADS + 4 * ML_INNER
    pad = SMALL_WIDTH - SSD_HEADS - 2 * ML_HEADS
    w_small = jnp.concatenate([w_t[o_dt:o_dt + SSD_HEADS], w_t[o_i:o_i + 2 * ML_HEADS],
                               jnp.zeros((pad, D_MODEL), F32)], axis=0)

    main, small = _in_proj(x2, norm_mix_w[layer].reshape(1, D_MODEL), w_main, w_small)

    zeros = jnp.zeros((SMALL_WIDTH - F_COL - ML_HEADS,), F32)
    bias_row = jnp.concatenate([dt_bias[layer], i_bias[layer], f_bias[layer], zeros]).reshape(1, SMALL_WIDTH)
    alog_row = jnp.concatenate([a_log[layer], jnp.zeros((SMALL_WIDTH - SSD_HEADS,), F32)]).reshape(1, SMALL_WIDTH)
    dskip_row = jnp.repeat(d_skip[layer].astype(F32), SSD_HEAD_DIM).reshape(1, SSD_INNER)

    y, hm = _mixers(
        main, small,
        conv_ssd_w[layer], conv_ssd_b[layer].reshape(1, SSD_XBC),
        conv_qk_w[layer], conv_qk_b[layer].reshape(1, 2 * ML_INNER),
        bias_row, alog_row, dskip_row,
        ssd_norm_w[layer].reshape(1, SSD_INNER), mlstm_norm_w[layer].reshape(1, ML_INNER),
        batch, seq)

    out = _merge(x2, y, hm, main,
                 w_br_ssd[layer].astype(BF16), w_br_mlstm[layer].astype(BF16), w_out[layer].astype(BF16),
                 norm_mlp_w[layer].reshape(1, D_MODEL), w_up[layer].astype(BF16), w_down[layer].astype(BF16),
                 norm_final_w.reshape(1, D_MODEL))
    return out.reshape(batch, seq, D_MODEL)
```

```python
import functools

import jax
import jax.numpy as jnp
import numpy as np
from jax import lax
from jax.experimental import pallas as pl
from jax.experimental.pallas import tpu as pltpu

F32 = jnp.float32
BF16 = jnp.bfloat16

LOG2E = 1.4426950408889634

D_MODEL = 1024
NORM_EPS = 1e-5
CONV_TAPS = 4
SSD_INNER = 2048
SSD_HEAD_DIM = 64
SSD_HEADS = 32
SSD_GROUPS = 4
SSD_STATE = 128
SSD_GROUP_WIDTH = SSD_INNER // SSD_GROUPS
SSD_XBC = SSD_INNER + 2 * SSD_GROUPS * SSD_STATE
ML_INNER = 1024
ML_HEADS = 4
ML_HEAD_DIM = 256
D_FF = 4096
IN_PROJ_WIDTH = SSD_INNER + SSD_XBC + SSD_HEADS + 4 * ML_INNER + 2 * ML_HEADS + 2 * D_MODEL

LANES = 128
BF16_ROWS = 16

COL_BLOCK = 1024
N_COL_BLOCKS = 11
MAIN_WIDTH = N_COL_BLOCKS * COL_BLOCK
Q_FIRST_BLOCK, G_FIRST_BLOCK = 5, 9
Z_BLOCKS, O_BLOCK = 2, 8
Q_ROW_OFFSET = SSD_HEADS
G_ROW_OFFSET = SSD_HEADS + 2 * ML_HEADS
SMALL_WIDTH = LANES
DT_COL, I_COL, F_COL = 0, 32, 36
DT_ROW_COPY = 64

CONV_WIDTH = SSD_XBC + 2 * ML_INNER
SCAN_CHUNK = 128
MIX_TOKENS = 256
CONV_PIECES = (112, 112, 32)
CONV_STRIP = 256
PROJ_TM = 2048
MERGE_TM = 512
HEADS_PER_DOT = 4
VMEM_LIMIT = 56 * 1024 * 1024


def _sigmoid(x):
    return 0.5 * jnp.tanh(0.5 * x) + 0.5


def _softplus(x):
    return jnp.maximum(x, 0.0) + jnp.log1p(jnp.exp(-jnp.abs(x)))


def _dot(a, b):
    return jnp.dot(a, b, preferred_element_type=F32)


def _dot_nt(a, b):
    return lax.dot_general(a, b, (((1,), (1,)), ((), ())), preferred_element_type=F32)


def _w_prep_kernel(w_ref, out_ref):
    j = pl.program_id(0)
    halved = (j < Z_BLOCKS) | (j == O_BLOCK)
    out_ref[...] = (w_ref[...] * jnp.where(halved, 0.5, 1.0)).astype(BF16)


def _w_prep(w_t):
    def src_rows(j):
        off = jnp.where(j < Q_FIRST_BLOCK, 0, jnp.where(j < G_FIRST_BLOCK, Q_ROW_OFFSET, G_ROW_OFFSET))
        return (pl.multiple_of(j * COL_BLOCK + off, 8), 0)

    return pl.pallas_call(
        _w_prep_kernel,
        grid=(N_COL_BLOCKS,),
        in_specs=[pl.BlockSpec((pl.Element(COL_BLOCK), pl.Element(D_MODEL)), src_rows)],
        out_specs=pl.BlockSpec((COL_BLOCK, D_MODEL), lambda j: (j, 0)),
        out_shape=jax.ShapeDtypeStruct((MAIN_WIDTH, D_MODEL), BF16),
        compiler_params=pltpu.CompilerParams(
            dimension_semantics=("arbitrary",),
            vmem_limit_bytes=VMEM_LIMIT),
        name="w_prep",
    )(w_t)


def _in_proj_kernel(x_ref, nw_ref, w_ref, ws_ref, main_ref, small_ref, u_ref):
    j = pl.program_id(1)

    @pl.when(j == 0)
    def _():
        x = x_ref[...]
        ms = jnp.mean(x * x, axis=-1, keepdims=True)
        u = (x * lax.rsqrt(ms + NORM_EPS) * nw_ref[...]).astype(BF16)
        u_ref[...] = u
        small_ref[...] = _dot_nt(u, ws_ref[...].astype(BF16))

    main_ref[...] = _dot_nt(u_ref[...], w_ref[...]).astype(BF16)


def _in_proj(x2, norm_w, w_main, w_small):
    tokens = x2.shape[0]
    grid = (tokens // PROJ_TM, N_COL_BLOCKS)
    return pl.pallas_call(
        _in_proj_kernel,
        grid=grid,
        in_specs=[
            pl.BlockSpec((PROJ_TM, D_MODEL), lambda i, j: (i, 0)),
            pl.BlockSpec((1, D_MODEL), lambda i, j: (0, 0)),
            pl.BlockSpec((COL_BLOCK, D_MODEL), lambda i, j: (j, 0)),
            pl.BlockSpec((SMALL_WIDTH, D_MODEL), lambda i, j: (0, 0)),
        ],
        out_specs=[
            pl.BlockSpec((PROJ_TM, COL_BLOCK), lambda i, j: (i, j)),
            pl.BlockSpec((PROJ_TM, SMALL_WIDTH), lambda i, j: (i, 0)),
        ],
        out_shape=[
            jax.ShapeDtypeStruct((tokens, MAIN_WIDTH), BF16),
            jax.ShapeDtypeStruct((tokens, SMALL_WIDTH), F32),
        ],
        scratch_shapes=[pltpu.VMEM((PROJ_TM, D_MODEL), BF16)],
        compiler_params=pltpu.CompilerParams(
            dimension_semantics=("arbitrary", "arbitrary"),
            vmem_limit_bytes=VMEM_LIMIT),
        name="in_proj",
    )(x2, norm_w, w_main, w_small)


def _shift_select_matrix(p):
    h = BF16_ROWS
    sel = np.zeros((p, CONV_TAPS * (p + h)), np.float32)
    for tap in range(CONV_TAPS):
        for t in range(p):
            sel[t, tap * (p + h) + t + h - (CONV_TAPS - 1 - tap)] = 1.0
    return jnp.asarray(sel, BF16)


def _head_expand_matrix():
    e = np.zeros((2 * LANES, SSD_INNER), np.float32)
    for h in range(SSD_HEADS):
        e[h, h * SSD_HEAD_DIM:(h + 1) * SSD_HEAD_DIM] = 1.0
        e[LANES + h, h * SSD_HEAD_DIM:(h + 1) * SSD_HEAD_DIM] = 1.0
    return jnp.asarray(e, BF16)


def _mixer_kernel(n_blocks, z_ref, xs_ref, bc_ref, q_ref, k_ref, v_ref, o_ref, sm_ref,
                  cw_ssd_ref, cb_ssd_ref, cw_qk_ref, cb_qk_ref,
                  bias_ref, alog_ref, dskip_ref, ssd_nw_ref, ml_nw_ref, sel_a_ref, sel_b_ref, e2_ref, eye_ref,
                  y_ref, hm_ref,
                  hist_ref, conv0_ref, conv1_ref,
                  s_ref, c_ref, n_ref, m_ref, p_ref, pt_ref, dt_ref):
    tb, L = MIX_TOKENS, SCAN_CHUNK
    n_chunks = tb // L
    hp = HEADS_PER_DOT
    qw = hp * SSD_HEAD_DIM
    gw = SSD_GROUP_WIDTH
    step = pl.program_id(1)
    bc_off, q_off, k_off = SSD_INNER, SSD_XBC, SSD_XBC + ML_INNER

    @pl.when(step == 0)
    def _():
        hist_ref[...] = jnp.zeros_like(hist_ref)
        s_ref[...] = jnp.zeros_like(s_ref)
        c_ref[...] = jnp.zeros_like(c_ref)
        n_ref[...] = jnp.zeros_like(n_ref)
        m_ref[...] = jnp.zeros_like(m_ref)

    def conv_items(dst_ref):
        h = BF16_ROWS
        sels = {CONV_PIECES[0]: sel_a_ref, CONV_PIECES[-1]: sel_b_ref}
        sources = ((xs_ref, 0, SSD_INNER, cw_ssd_ref, cb_ssd_ref, 0),
                   (bc_ref, bc_off, SSD_XBC - SSD_INNER, cw_ssd_ref, cb_ssd_ref, SSD_INNER),
                   (q_ref, q_off, ML_INNER, cw_qk_ref, cb_qk_ref, 0),
                   (k_ref, k_off, ML_INNER, cw_qk_ref, cb_qk_ref, ML_INNER))

        def strip(src_ref, dst_off, w_ref, b_ref, w_off, c0):
            cols = slice(c0, c0 + CONV_STRIP)
            wcols = slice(w_off + c0, w_off + c0 + CONV_STRIP)
            dcols = slice(dst_off + c0, dst_off + c0 + CONV_STRIP)
            taps = [jnp.broadcast_to(0.5 * w_ref[tap:tap + 1, wcols], (h, CONV_STRIP)).astype(BF16)
                    for tap in range(CONV_TAPS)]
            half_bias = 0.5 * b_ref[:, wcols]
            r0 = 0
            for p in CONV_PIECES:
                if r0 == 0:
                    win = jnp.concatenate([hist_ref[:, dcols], src_ref[0:p, cols]], axis=0)
                else:
                    win = src_ref[r0 - h:r0 + p, cols]
                win = win.reshape((p + h) // h, h, CONV_STRIP)
                scaled = jnp.concatenate([(win * taps[tap][None]).reshape(p + h, CONV_STRIP)
                                          for tap in range(CONV_TAPS)], axis=0)
                g = _dot(sels[p][...], scaled) + half_bias
                dst_ref[r0:r0 + p, dcols] = (g * jnp.tanh(g) + g).astype(BF16)
                r0 += p
            hist_ref[:, dcols] = src_ref[tb - h:tb, cols]

        items = []
        for src_ref, dst_off, width, w_ref, b_ref, w_off in sources:
            for c0 in range(0, width, CONV_STRIP):
                items.append(lambda a=(src_ref, dst_off, w_ref, b_ref, w_off, c0): strip(*a))
        return items

    def scans(cv_ref, pending):
        pending = list(pending)
        n_slots = n_chunks * (SSD_GROUPS + 3)
        per_slot = -(-len(pending) // n_slots)

        def emit():
            for _ in range(per_slot):
                if pending:
                    pending.pop(0)()

        sm = sm_ref[...] + bias_ref[...]
        lane = lax.broadcasted_iota(jnp.int32, (tb, SMALL_WIDTH), 1)
        dt = _softplus(sm)
        log_f = -_softplus(-sm)
        a_row = -jnp.exp(alog_ref[...])
        is_dt = lane < I_COL
        is_i = (lane >= I_COL) & (lane < F_COL)
        is_f = (lane >= F_COL) & (lane < F_COL + ML_HEADS)
        pre = jnp.where(is_dt, dt * a_row, jnp.where(is_f, log_f, 0.0))
        tri = (lax.broadcasted_iota(jnp.int32, (L, L), 1)
               <= lax.broadcasted_iota(jnp.int32, (L, L), 0)).astype(F32)
        cs = jnp.concatenate(
            [jnp.dot(tri, pre[c * L:(c + 1) * L, :], preferred_element_type=F32, precision=lax.Precision.HIGHEST)
             for c in range(n_chunks)], axis=0)
        dt_copy = pltpu.roll(dt, DT_ROW_COPY, axis=1)
        is_dt_copy = (lane >= DT_ROW_COPY) & (lane < DT_ROW_COPY + SSD_HEADS)
        table = jnp.where(is_dt | is_f, cs * LOG2E,
                          jnp.where(is_i, sm * LOG2E, jnp.where(is_dt_copy, dt_copy, 0.0)))
        p_ref[...] = table
        dt_ref[...] = jnp.where(is_dt, dt, 0.0)
        for c in range(n_chunks):
            pt_ref[c] = table[c * L:(c + 1) * L, :].T

        causal = (lax.broadcasted_iota(jnp.int32, (L, L), 0) >= lax.broadcasted_iota(jnp.int32, (L, L), 1))
        dt_lanes = lax.broadcasted_iota(jnp.int32, (L, SMALL_WIDTH), 1) < I_COL
        lane_q = lax.broadcasted_iota(jnp.int32, (1, qw), 1)
        head_masks = [(lane_q >= a * SSD_HEAD_DIM) & (lane_q < (a + 1) * SSD_HEAD_DIM) for a in range(hp)]
        neg_inf = jnp.float32(-jnp.inf)

        def split_hi_lo(v):
            hi = v.astype(BF16)
            lo = (v - hi.astype(F32)).astype(BF16)
            return jnp.concatenate([hi, lo], axis=1)

        for c in range(n_chunks):
            r0 = c * L
            rows = slice(r0, r0 + L)
            tab = p_ref[rows, :]
            tab_end = p_ref[r0 + L - 1:r0 + L, :]
            tab_t = pt_ref[c]
            dt_blk = dt_ref[rows, :]
            ea2 = split_hi_lo(jnp.where(dt_lanes, jnp.exp2(tab), 0.0))
            td2 = split_hi_lo(jnp.where(dt_lanes, jnp.exp2(tab_end - tab) * dt_blk, 0.0))

            for g in range(SSD_GROUPS):
                emit()
                gcols = slice(g * gw, (g + 1) * gw)
                b_mat = cv_ref[rows, bc_off + g * SSD_STATE:bc_off + (g + 1) * SSD_STATE]
                c_mat = cv_ref[rows, bc_off + (SSD_GROUPS + g) * SSD_STATE:
                               bc_off + (SSD_GROUPS + g + 1) * SSD_STATE]
                cb = _dot_nt(c_mat, b_mat)
                ea_g = _dot(ea2, e2_ref[:, gcols])
                td_g = _dot(td2, e2_ref[:, gcols])
                x_g = cv_ref[rows, gcols]
                x_gf = x_g.astype(F32)
                s_g = s_ref[:, gcols]
                inter = _dot(c_mat, s_g.astype(BF16)) * ea_g
                intra_parts = []
                for qd in range(gw // qw):
                    x_q = x_g[:, qd * qw:(qd + 1) * qw]
                    zero_x = jnp.zeros_like(x_q)
                    lhs_parts, rhs_parts = [], []
                    for a in range(hp):
                        h = (g * gw + qd * qw) // SSD_HEAD_DIM + a
                        seg = tab[:, h:h + 1] - tab_t[h:h + 1, :]
                        dec = jnp.exp2(jnp.where(causal, seg, neg_inf))
                        w_mat = cb * dec * tab_t[DT_ROW_COPY + h:DT_ROW_COPY + h + 1, :]
                        lhs_parts.append(w_mat.astype(BF16))
                        rhs_parts.append(jnp.where(head_masks[a], x_q, zero_x))
                    intra_parts.append(_dot(jnp.concatenate(lhs_parts, axis=1),
                                            jnp.concatenate(rhs_parts, axis=0)))
                y_g = jnp.concatenate(intra_parts, axis=1) + inter + dskip_ref[:, gcols] * x_gf
                zh = z_ref[rows, gcols].astype(F32)
                yz = y_g * (zh * jnp.tanh(zh) + zh)
                ms = jnp.mean(yz * yz, axis=-1, keepdims=True)
                y_ref[rows, gcols] = (yz * lax.rsqrt(ms + NORM_EPS) * ssd_nw_ref[:, gcols]).astype(BF16)
                xw = (x_gf * td_g).astype(BF16)
                b_t = _dot_nt(eye_ref[0:SSD_STATE, 0:SSD_STATE], b_mat).astype(BF16)
                s_ref[:, gcols] = s_g * ea_g[L - 1:L, :] + _dot(b_t, xw)

            emit()
            heads = range(ML_HEADS)
            hcols = [slice(h * ML_HEAD_DIM, (h + 1) * ML_HEAD_DIM) for h in heads]
            q_hs = [cv_ref[rows, q_off + h * ML_HEAD_DIM:q_off + (h + 1) * ML_HEAD_DIM]
                    * jnp.asarray(ML_HEAD_DIM ** -0.5, BF16) for h in heads]
            k_hs = [cv_ref[rows, k_off + h * ML_HEAD_DIM:k_off + (h + 1) * ML_HEAD_DIM] for h in heads]
            v_hs = [v_ref[rows, hcols[h]] for h in heads]
            c_prevs = [c_ref[h] for h in heads]
            n_prevs = [n_ref[h] for h in heads]
            qk_raw = [_dot_nt(q_hs[h], k_hs[h]) for h in heads]
            q_c_prev = [_dot(q_hs[h], c_prevs[h].astype(BF16)) for h in heads]
            k_ts = [_dot_nt(eye_ref[...], k_hs[h]) for h in heads]
            w_intra, w_inter, m_ts, wg2s, kw_ts, m_news, a_olds = [], [], [], [], [], [], []
            for h in heads:
                b_col = tab[:, F_COL + h:F_COL + h + 1]
                b_row = tab_t[F_COL + h:F_COL + h + 1, :]
                li_row = tab_t[I_COL + h:I_COL + h + 1, :]
                b_end = tab_end[:, F_COL + h:F_COL + h + 1]
                m_prev = m_ref[0:1, h:h + 1]
                d_log = jnp.where(causal, b_col - b_row + li_row, neg_inf)
                inter_log = b_col + m_prev
                m_t = jnp.maximum(inter_log, jnp.max(d_log, axis=1, keepdims=True))
                w_intra.append(jnp.exp2(d_log - m_t))
                w_inter.append(jnp.exp2(inter_log - m_t))
                m_ts.append(m_t)
                g_row = b_end - b_row + li_row
                m_loc = jnp.max(g_row, axis=1, keepdims=True)
                m_new = jnp.maximum(b_end + m_prev, m_loc)
                m_news.append(m_new)
                a_olds.append(jnp.exp2(b_end + m_prev - m_new))
                wg_row = jnp.exp2(g_row - m_new)
                kw_ts.append((k_ts[h] * wg_row).astype(BF16))
                wg2s.append(split_hi_lo(wg_row))
            emit()
            c_locs = [_dot(kw_ts[h], v_hs[h]) for h in heads]
            n_locs = [_dot(wg2s[h], jnp.concatenate([k_hs[h], k_hs[h]], axis=0)) for h in heads]
            qks = [qk_raw[h] * w_intra[h] for h in heads]
            nums = [_dot(qks[h].astype(BF16), v_hs[h]) + w_inter[h] * q_c_prev[h] for h in heads]
            emit()
            for h in heads:
                den = (jnp.sum(qks[h], axis=1, keepdims=True)
                       + w_inter[h] * jnp.sum(q_hs[h].astype(F32) * n_prevs[h], axis=1, keepdims=True))
                hh = nums[h] / jnp.maximum(jnp.abs(den), jnp.exp2(-m_ts[h]))
                ms = jnp.mean(hh * hh, axis=-1, keepdims=True)
                hn = hh * lax.rsqrt(ms + NORM_EPS) * ml_nw_ref[:, hcols[h]]
                gate = 0.5 * jnp.tanh(o_ref[rows, hcols[h]].astype(F32)) + 0.5
                hm_ref[rows, hcols[h]] = (gate * hn).astype(BF16)
            for h in heads:
                c_ref[h] = a_olds[h] * c_prevs[h] + c_locs[h]
                n_ref[h] = a_olds[h] * n_prevs[h] + n_locs[h]
                m_ref[0:1, h:h + 1] = m_news[h]
        while pending:
            pending.pop(0)()

    bufs = (conv0_ref, conv1_ref)

    @pl.when(step == 0)
    def _():
        for item in conv_items(bufs[0]):
            item()

    for parity in range(2):
        @pl.when((step > 0) & (step < n_blocks) & (step % 2 == parity))
        def _():
            scans(bufs[1 - parity], conv_items(bufs[parity]))

    @pl.when(step == n_blocks)
    def _():
        scans(bufs[(n_blocks - 1) % 2], [])


def _mixers(main, small, cw_ssd, cb_ssd, cw_qk, cb_qk,
            bias_row, alog_row, dskip_row, ssd_nw, ml_nw, batch, seq):
    tb, L = MIX_TOKENS, SCAN_CHUNK
    spb = seq // tb
    tokens = batch * seq
    assert sum(CONV_PIECES) == tb and len(set(CONV_PIECES)) == 2
    sel_a = _shift_select_matrix(CONV_PIECES[0])
    sel_b = _shift_select_matrix(CONV_PIECES[-1])
    e2 = _head_expand_matrix()
    eye = jnp.eye(ML_HEAD_DIM, dtype=BF16)

    def ahead(width, idx):
        return pl.BlockSpec((tb, width), lambda b, s: (b * spb + jnp.minimum(s, spb - 1), idx))

    def behind(width, idx):
        return pl.BlockSpec((tb, width), lambda b, s: (b * spb + jnp.maximum(s - 1, 0), idx))

    def const(shape):
        return pl.BlockSpec(shape, lambda b, s: tuple(0 for _ in shape))

    in_specs = [
        behind(2 * COL_BLOCK, 0),
        ahead(2 * COL_BLOCK, 1),
        ahead(COL_BLOCK, 4),
        ahead(COL_BLOCK, 5),
        ahead(COL_BLOCK, 6),
        behind(COL_BLOCK, 7),
        behind(COL_BLOCK, 8),
        behind(SMALL_WIDTH, 0),
        const((CONV_TAPS, SSD_XBC)), const((1, SSD_XBC)),
        const((CONV_TAPS, 2 * ML_INNER)), const((1, 2 * ML_INNER)),
        const((1, SMALL_WIDTH)), const((1, SMALL_WIDTH)),
        const((1, SSD_INNER)), const((1, SSD_INNER)), const((1, ML_INNER)),
        const(sel_a.shape), const(sel_b.shape), const(e2.shape), const(eye.shape),
    ]
    out_specs = [behind(SSD_INNER, 0), behind(ML_INNER, 0)]
    scratch = [
        pltpu.VMEM((BF16_ROWS, CONV_WIDTH), BF16),
        pltpu.VMEM((tb, CONV_WIDTH), BF16),
        pltpu.VMEM((tb, CONV_WIDTH), BF16),
        pltpu.VMEM((SSD_STATE, SSD_INNER), F32),
        pltpu.VMEM((ML_HEADS, ML_HEAD_DIM, ML_HEAD_DIM), F32),
        pltpu.VMEM((ML_HEADS, 1, ML_HEAD_DIM), F32),
        pltpu.VMEM((8, LANES), F32),
        pltpu.VMEM((tb, SMALL_WIDTH), F32),
        pltpu.VMEM((tb // L, SMALL_WIDTH, L), F32),
        pltpu.VMEM((tb, SMALL_WIDTH), F32),
    ]
    return pl.pallas_call(
        functools.partial(_mixer_kernel, spb),
        grid=(batch, spb + 1),
        in_specs=in_specs,
        out_specs=out_specs,
        out_shape=[jax.ShapeDtypeStruct((tokens, SSD_INNER), BF16),
                   jax.ShapeDtypeStruct((tokens, ML_INNER), BF16)],
        scratch_shapes=scratch,
        compiler_params=pltpu.CompilerParams(
            dimension_semantics=("arbitrary", "arbitrary"),
            vmem_limit_bytes=VMEM_LIMIT),
        name="mixers",
    )(main, main, main, main, main, main, main, small,
      cw_ssd, cb_ssd, cw_qk, cb_qk,
      bias_row, alog_row, dskip_row, ssd_nw, ml_nw, sel_a, sel_b, e2, eye)


def _rms(x, w):
    ms = jnp.mean(x * x, axis=-1, keepdims=True)
    return x * lax.rsqrt(ms + NORM_EPS) * w


def _merge_kernel(x_ref, y_ref, hm_ref, gs_ref, gm_ref,
                  wbs_ref, wbm_ref, wo_ref, nmw_ref, wup_ref, wdn_ref, nfw_ref, out_ref):
    a = _dot(y_ref[...], wbs_ref[...])
    b = _dot(hm_ref[...], wbm_ref[...])
    mixed = _sigmoid(gs_ref[...].astype(F32)) * a + _sigmoid(gm_ref[...].astype(F32)) * b
    h1 = x_ref[...] + _dot(mixed.astype(BF16), wo_ref[...])
    u = _rms(h1, nmw_ref[...]).astype(BF16)
    up = jnp.maximum(_dot(u, wup_ref[...]), 0.0)
    act = (up * up).astype(BF16)
    h2 = h1 + _dot(act, wdn_ref[...])
    out_ref[...] = _rms(h2, nfw_ref[...])


def _merge(x2, y, hm, main, w_br_ssd, w_br_ml, w_out, norm_mlp_w, w_up, w_down, norm_final_w):
    tokens = x2.shape[0]
    tm = MERGE_TM

    def resident(shape):
        return pl.BlockSpec(shape, lambda i: (0, 0), pipeline_mode=pl.Buffered(1))

    return pl.pallas_call(
        _merge_kernel,
        grid=(tokens // tm,),
        in_specs=[
            pl.BlockSpec((tm, D_MODEL), lambda i: (i, 0)),
            pl.BlockSpec((tm, SSD_INNER), lambda i: (i, 0)),
            pl.BlockSpec((tm, ML_INNER), lambda i: (i, 0)),
            pl.BlockSpec((tm, COL_BLOCK), lambda i: (i, 9)),
            pl.BlockSpec((tm, COL_BLOCK), lambda i: (i, 10)),
            resident((SSD_INNER, D_MODEL)),
            resident((ML_INNER, D_MODEL)),
            resident((D_MODEL, D_MODEL)),
            resident((1, D_MODEL)),
            resident((D_MODEL, D_FF)),
            resident((D_FF, D_MODEL)),
            resident((1, D_MODEL)),
        ],
        out_specs=pl.BlockSpec((tm, D_MODEL), lambda i: (i, 0)),
        out_shape=jax.ShapeDtypeStruct((tokens, D_MODEL), F32),
        compiler_params=pltpu.CompilerParams(
            dimension_semantics=("arbitrary",),
            vmem_limit_bytes=VMEM_LIMIT),
        name="merge_mlp",
    )(x2, y, hm, main, main, w_br_ssd, w_br_ml, w_out, norm_mlp_w, w_up, w_down, norm_final_w)


def kernel(x, norm_mix_w, w_in, conv_ssd_w, conv_ssd_b, dt_bias, a_log, d_skip, ssd_norm_w,
           conv_qk_w, conv_qk_b, i_bias, f_bias, mlstm_norm_w, w_br_ssd, w_br_mlstm, w_out,
           norm_mlp_w, w_up, w_down, norm_final_w):
    batch, seq, _ = x.shape
    x2 = x.reshape(batch * seq, D_MODEL)
    layer = 0

    w_t = jnp.transpose(w_in[layer])
    w_main = _w_prep(w_t)
    o_dt = SSD_INNER + SSD_XBC
    o_i = o_dt + SSD_HEADS + 4 * ML_INNER
    pad = SMALL_WIDTH - SSD_HEADS - 2 * ML_HEADS
    w_small = jnp.concatenate([w_t[o_dt:o_dt + SSD_HEADS], w_t[o_i:o_i + 2 * ML_HEADS],
                               jnp.zeros((pad, D_MODEL), F32)], axis=0)

    main, small = _in_proj(x2, norm_mix_w[layer].reshape(1, D_MODEL), w_main, w_small)

    zeros = jnp.zeros((SMALL_WIDTH - F_COL - ML_HEADS,), F32)
    bias_row = jnp.concatenate([dt_bias[layer], i_bias[layer], f_bias[layer], zeros]).reshape(1, SMALL_WIDTH)
    alog_row = jnp.concatenate([a_log[layer], jnp.zeros((SMALL_WIDTH - SSD_HEADS,), F32)]).reshape(1, SMALL_WIDTH)
    dskip_row = jnp.repeat(d_skip[layer].astype(F32), SSD_HEAD_DIM).reshape(1, SSD_INNER)

    y, hm = _mixers(
        main, small,
        conv_ssd_w[layer], conv_ssd_b[layer].reshape(1, SSD_XBC),
        conv_qk_w[layer], conv_qk_b[layer].reshape(1, 2 * ML_INNER),
        bias_row, alog_row, dskip_row,
        ssd_norm_w[layer].reshape(1, SSD_INNER), mlstm_norm_w[layer].reshape(1, ML_INNER),
        batch, seq)

    out = _merge(x2, y, hm, main,
                 w_br_ssd[layer].astype(BF16), w_br_mlstm[layer].astype(BF16), w_out[layer].astype(BF16),
                 norm_mlp_w[layer].reshape(1, D_MODEL), w_up[layer].astype(BF16), w_down[layer].astype(BF16),
                 norm_final_w.reshape(1, D_MODEL))
    return out.reshape(batch, seq, D_MODEL)
```

```python
import functools

import jax
import jax.numpy as jnp
import numpy as np
from jax import lax
from jax.experimental import pallas as pl
from jax.experimental.pallas import tpu as pltpu

F32 = jnp.float32
BF16 = jnp.bfloat16

LOG2E = 1.4426950408889634

D_MODEL = 1024
NORM_EPS = 1e-5
CONV_TAPS = 4
SSD_INNER = 2048
SSD_HEAD_DIM = 64
SSD_HEADS = 32
SSD_GROUPS = 4
SSD_STATE = 128
SSD_GROUP_WIDTH = SSD_INNER // SSD_GROUPS
SSD_XBC = SSD_INNER + 2 * SSD_GROUPS * SSD_STATE
ML_INNER = 1024
ML_HEADS = 4
ML_HEAD_DIM = 256
D_FF = 4096
IN_PROJ_WIDTH = SSD_INNER + SSD_XBC + SSD_HEADS + 4 * ML_INNER + 2 * ML_HEADS + 2 * D_MODEL

LANES = 128
BF16_ROWS = 16

COL_BLOCK = 1024
N_COL_BLOCKS = 11
MAIN_WIDTH = N_COL_BLOCKS * COL_BLOCK
Q_FIRST_BLOCK, G_FIRST_BLOCK = 5, 9
Z_BLOCKS, O_BLOCK = 2, 8
Q_ROW_OFFSET = SSD_HEADS
G_ROW_OFFSET = SSD_HEADS + 2 * ML_HEADS
SMALL_WIDTH = LANES
DT_COL, I_COL, F_COL = 0, 32, 36
DT_ROW_COPY = 64

CONV_WIDTH = SSD_XBC + 2 * ML_INNER
SCAN_CHUNK = 128
MIX_TOKENS = 256
CONV_PIECES = (112, 112, 32)
CONV_STRIP = 256
PROJ_TM = 2048
MERGE_TM = 512
HEADS_PER_DOT = 4
VMEM_LIMIT = 56 * 1024 * 1024


def _sigmoid(x):
    return 0.5 * jnp.tanh(0.5 * x) + 0.5


def _softplus(x):
    return jnp.maximum(x, 0.0) + jnp.log1p(jnp.exp(-jnp.abs(x)))


def _dot(a, b):
    return jnp.dot(a, b, preferred_element_type=F32)


def _dot_nt(a, b):
    return lax.dot_general(a, b, (((1,), (1,)), ((), ())), preferred_element_type=F32)


def _w_prep_kernel(w_ref, out_ref):
    j = pl.program_id(0)
    halved = (j < Z_BLOCKS) | (j == O_BLOCK)
    out_ref[...] = (w_ref[...] * jnp.where(halved, 0.5, 1.0)).astype(BF16)


def _w_prep(w_t):
    def src_rows(j):
        off = jnp.where(j < Q_FIRST_BLOCK, 0, jnp.where(j < G_FIRST_BLOCK, Q_ROW_OFFSET, G_ROW_OFFSET))
        return (pl.multiple_of(j * COL_BLOCK + off, 8), 0)

    return pl.pallas_call(
        _w_prep_kernel,
        grid=(N_COL_BLOCKS,),
        in_specs=[pl.BlockSpec((pl.Element(COL_BLOCK), pl.Element(D_MODEL)), src_rows)],
        out_specs=pl.BlockSpec((COL_BLOCK, D_MODEL), lambda j: (j, 0)),
        out_shape=jax.ShapeDtypeStruct((MAIN_WIDTH, D_MODEL), BF16),
        compiler_params=pltpu.CompilerParams(
            dimension_semantics=("arbitrary",),
            vmem_limit_bytes=VMEM_LIMIT),
        name="w_prep",
    )(w_t)


def _in_proj_kernel(x_ref, nw_ref, w_ref, ws_ref, main_ref, small_ref, u_ref):
    j = pl.program_id(1)

    @pl.when(j == 0)
    def _():
        x = x_ref[...]
        ms = jnp.mean(x * x, axis=-1, keepdims=True)
        u = (x * lax.rsqrt(ms + NORM_EPS) * nw_ref[...]).astype(BF16)
        u_ref[...] = u
        small_ref[...] = _dot_nt(u, ws_ref[...].astype(BF16))

    main_ref[...] = _dot_nt(u_ref[...], w_ref[...]).astype(BF16)


def _in_proj(x2, norm_w, w_main, w_small):
    tokens = x2.shape[0]
    grid = (tokens // PROJ_TM, N_COL_BLOCKS)
    return pl.pallas_call(
        _in_proj_kernel,
        grid=grid,
        in_specs=[
            pl.BlockSpec((PROJ_TM, D_MODEL), lambda i, j: (i, 0)),
            pl.BlockSpec((1, D_MODEL), lambda i, j: (0, 0)),
            pl.BlockSpec((COL_BLOCK, D_MODEL), lambda i, j: (j, 0)),
            pl.BlockSpec((SMALL_WIDTH, D_MODEL), lambda i, j: (0, 0)),
        ],
        out_specs=[
            pl.BlockSpec((PROJ_TM, COL_BLOCK), lambda i, j: (i, j)),
            pl.BlockSpec((PROJ_TM, SMALL_WIDTH), lambda i, j: (i, 0)),
        ],
        out_shape=[
            jax.ShapeDtypeStruct((tokens, MAIN_WIDTH), BF16),
            jax.ShapeDtypeStruct((tokens, SMALL_WIDTH), F32),
        ],
        scratch_shapes=[pltpu.VMEM((PROJ_TM, D_MODEL), BF16)],
        compiler_params=pltpu.CompilerParams(
            dimension_semantics=("arbitrary", "arbitrary"),
            vmem_limit_bytes=VMEM_LIMIT),
        name="in_proj",
    )(x2, norm_w, w_main, w_small)


def _shift_select_matrix(p):
    h = BF16_ROWS
    sel = np.zeros((p, CONV_TAPS * (p + h)), np.float32)
    for tap in range(CONV_TAPS):
        for t in range(p):
            sel[t, tap * (p + h) + t + h - (CONV_TAPS - 1 - tap)] = 1.0
    return jnp.asarray(sel, BF16)


def _head_expand_matrix():
    e = np.zeros((2 * LANES, SSD_INNER), np.float32)
    for h in range(SSD_HEADS):
        e[h, h * SSD_HEAD_DIM:(h + 1) * SSD_HEAD_DIM] = 1.0
        e[LANES + h, h * SSD_HEAD_DIM:(h + 1) * SSD_HEAD_DIM] = 1.0
    return jnp.asarray(e, BF16)


def _mixer_kernel(n_blocks, seq_blocks, z_ref, xs_ref, bc_ref, q_ref, k_ref, v_ref, o_ref, sm_ref,
                  cw_ssd_ref, cb_ssd_ref, cw_qk_ref, cb_qk_ref,
                  bias_ref, alog_ref, dskip_ref, ssd_nw_ref, ml_nw_ref, sel_a_ref, sel_b_ref, e2_ref, eye_ref,
                  y_ref, hm_ref,
                  hist_ref, conv0_ref, conv1_ref,
                  s_ref, c_ref, n_ref, m_ref, p_ref, pt_ref, dt_ref):
    tb, L = MIX_TOKENS, SCAN_CHUNK
    n_chunks = tb // L
    hp = HEADS_PER_DOT
    qw = hp * SSD_HEAD_DIM
    gw = SSD_GROUP_WIDTH
    step = pl.program_id(0)
    bc_off, q_off, k_off = SSD_INNER, SSD_XBC, SSD_XBC + ML_INNER

    @pl.when(step % seq_blocks == 0)
    def _():
        hist_ref[...] = jnp.zeros_like(hist_ref)

    @pl.when(step % seq_blocks == 1 % seq_blocks)
    def _():
        s_ref[...] = jnp.zeros_like(s_ref)
        c_ref[...] = jnp.zeros_like(c_ref)
        n_ref[...] = jnp.zeros_like(n_ref)
        m_ref[...] = jnp.zeros_like(m_ref)

    def conv_items(dst_ref):
        h = BF16_ROWS
        sels = {CONV_PIECES[0]: sel_a_ref, CONV_PIECES[-1]: sel_b_ref}
        sources = ((xs_ref, 0, SSD_INNER, cw_ssd_ref, cb_ssd_ref, 0),
                   (bc_ref, bc_off, SSD_XBC - SSD_INNER, cw_ssd_ref, cb_ssd_ref, SSD_INNER),
                   (q_ref, q_off, ML_INNER, cw_qk_ref, cb_qk_ref, 0),
                   (k_ref, k_off, ML_INNER, cw_qk_ref, cb_qk_ref, ML_INNER))

        def strip(src_ref, dst_off, w_ref, b_ref, w_off, c0):
            cols = slice(c0, c0 + CONV_STRIP)
            wcols = slice(w_off + c0, w_off + c0 + CONV_STRIP)
            dcols = slice(dst_off + c0, dst_off + c0 + CONV_STRIP)
            taps = [jnp.broadcast_to(0.5 * w_ref[tap:tap + 1, wcols], (h, CONV_STRIP)).astype(BF16)
                    for tap in range(CONV_TAPS)]
            half_bias = 0.5 * b_ref[:, wcols]
            r0 = 0
            for p in CONV_PIECES:
                if r0 == 0:
                    win = jnp.concatenate([hist_ref[:, dcols], src_ref[0:p, cols]], axis=0)
                else:
                    win = src_ref[r0 - h:r0 + p, cols]
                win = win.reshape((p + h) // h, h, CONV_STRIP)
                scaled = jnp.concatenate([(win * taps[tap][None]).reshape(p + h, CONV_STRIP)
                                          for tap in range(CONV_TAPS)], axis=0)
                g = _dot(sels[p][...], scaled) + half_bias
                dst_ref[r0:r0 + p, dcols] = (g * jnp.tanh(g) + g).astype(BF16)
                r0 += p
            hist_ref[:, dcols] = src_ref[tb - h:tb, cols]

        items = []
        for src_ref, dst_off, width, w_ref, b_ref, w_off in sources:
            for c0 in range(0, width, CONV_STRIP):
                items.append(lambda a=(src_ref, dst_off, w_ref, b_ref, w_off, c0): strip(*a))
        return items

    def scans(cv_ref, pending):
        pending = list(pending)
        n_slots = n_chunks * (SSD_GROUPS + 3)
        per_slot = -(-len(pending) // n_slots)

        def emit():
            for _ in range(per_slot):
                if pending:
                    pending.pop(0)()

        sm = sm_ref[...] + bias_ref[...]
        lane = lax.broadcasted_iota(jnp.int32, (tb, SMALL_WIDTH), 1)
        dt = _softplus(sm)
        log_f = -_softplus(-sm)
        a_row = -jnp.exp(alog_ref[...])
        is_dt = lane < I_COL
        is_i = (lane >= I_COL) & (lane < F_COL)
        is_f = (lane >= F_COL) & (lane < F_COL + ML_HEADS)
        pre = jnp.where(is_dt, dt * a_row, jnp.where(is_f, log_f, 0.0))
        tri = (lax.broadcasted_iota(jnp.int32, (L, L), 1)
               <= lax.broadcasted_iota(jnp.int32, (L, L), 0)).astype(F32)
        cs = jnp.concatenate(
            [jnp.dot(tri, pre[c * L:(c + 1) * L, :], preferred_element_type=F32, precision=lax.Precision.HIGHEST)
             for c in range(n_chunks)], axis=0)
        dt_copy = pltpu.roll(dt, DT_ROW_COPY, axis=1)
        is_dt_copy = (lane >= DT_ROW_COPY) & (lane < DT_ROW_COPY + SSD_HEADS)
        table = jnp.where(is_dt | is_f, cs * LOG2E,
                          jnp.where(is_i, sm * LOG2E, jnp.where(is_dt_copy, dt_copy, 0.0)))
        p_ref[...] = table
        dt_ref[...] = jnp.where(is_dt, dt, 0.0)
        for c in range(n_chunks):
            pt_ref[c] = table[c * L:(c + 1) * L, :].T

        causal = (lax.broadcasted_iota(jnp.int32, (L, L), 0) >= lax.broadcasted_iota(jnp.int32, (L, L), 1))
        dt_lanes = lax.broadcasted_iota(jnp.int32, (L, SMALL_WIDTH), 1) < I_COL
        lane_q = lax.broadcasted_iota(jnp.int32, (1, qw), 1)
        head_masks = [(lane_q >= a * SSD_HEAD_DIM) & (lane_q < (a + 1) * SSD_HEAD_DIM) for a in range(hp)]
        neg_inf = jnp.float32(-jnp.inf)

        def split_hi_lo(v):
            hi = v.astype(BF16)
            lo = (v - hi.astype(F32)).astype(BF16)
            return jnp.concatenate([hi, lo], axis=1)

        for c in range(n_chunks):
            r0 = c * L
            rows = slice(r0, r0 + L)
            tab = p_ref[rows, :]
            tab_end = p_ref[r0 + L - 1:r0 + L, :]
            tab_t = pt_ref[c]
            dt_blk = dt_ref[rows, :]
            ea2 = split_hi_lo(jnp.where(dt_lanes, jnp.exp2(tab), 0.0))
            td2 = split_hi_lo(jnp.where(dt_lanes, jnp.exp2(tab_end - tab) * dt_blk, 0.0))

            for g in range(SSD_GROUPS):
                emit()
                gcols = slice(g * gw, (g + 1) * gw)
                b_mat = cv_ref[rows, bc_off + g * SSD_STATE:bc_off + (g + 1) * SSD_STATE]
                c_mat = cv_ref[rows, bc_off + (SSD_GROUPS + g) * SSD_STATE:
                               bc_off + (SSD_GROUPS + g + 1) * SSD_STATE]
                cb = _dot_nt(c_mat, b_mat)
                ea_g = _dot(ea2, e2_ref[:, gcols])
                td_g = _dot(td2, e2_ref[:, gcols])
                x_g = cv_ref[rows, gcols]
                x_gf = x_g.astype(F32)
                s_g = s_ref[:, gcols]
                inter = _dot(c_mat, s_g.astype(BF16)) * ea_g
                intra_parts = []
                for qd in range(gw // qw):
                    x_q = x_g[:, qd * qw:(qd + 1) * qw]
                    zero_x = jnp.zeros_like(x_q)
                    lhs_parts, rhs_parts = [], []
                    for a in range(hp):
                        h = (g * gw + qd * qw) // SSD_HEAD_DIM + a
                        seg = tab[:, h:h + 1] - tab_t[h:h + 1, :]
                        dec = jnp.exp2(jnp.where(causal, seg, neg_inf))
                        w_mat = cb * dec * tab_t[DT_ROW_COPY + h:DT_ROW_COPY + h + 1, :]
                        lhs_parts.append(w_mat.astype(BF16))
                        rhs_parts.append(jnp.where(head_masks[a], x_q, zero_x))
                    intra_parts.append(_dot(jnp.concatenate(lhs_parts, axis=1),
                                            jnp.concatenate(rhs_parts, axis=0)))
                y_g = jnp.concatenate(intra_parts, axis=1) + inter + dskip_ref[:, gcols] * x_gf
                zh = z_ref[rows, gcols].astype(F32)
                yz = y_g * (zh * jnp.tanh(zh) + zh)
                ms = jnp.mean(yz * yz, axis=-1, keepdims=True)
                y_ref[rows, gcols] = (yz * lax.rsqrt(ms + NORM_EPS) * ssd_nw_ref[:, gcols]).astype(BF16)
                xw = (x_gf * td_g).astype(BF16)
                b_t = _dot_nt(eye_ref[0:SSD_STATE, 0:SSD_STATE], b_mat).astype(BF16)
                s_ref[:, gcols] = s_g * ea_g[L - 1:L, :] + _dot(b_t, xw)

            emit()
            heads = range(ML_HEADS)
            hcols = [slice(h * ML_HEAD_DIM, (h + 1) * ML_HEAD_DIM) for h in heads]
            q_hs = [cv_ref[rows, q_off + h * ML_HEAD_DIM:q_off + (h + 1) * ML_HEAD_DIM]
                    * jnp.asarray(ML_HEAD_DIM ** -0.5, BF16) for h in heads]
            k_hs = [cv_ref[rows, k_off + h * ML_HEAD_DIM:k_off + (h + 1) * ML_HEAD_DIM] for h in heads]
            v_hs = [v_ref[rows, hcols[h]] for h in heads]
            c_prevs = [c_ref[h] for h in heads]
            n_prevs = [n_ref[h] for h in heads]
            qk_raw = [_dot_nt(q_hs[h], k_hs[h]) for h in heads]
            q_c_prev = [_dot(q_hs[h], c_prevs[h].astype(BF16)) for h in heads]
            k_ts = [_dot_nt(eye_ref[...], k_hs[h]) for h in heads]
            w_intra, w_inter, m_ts, wg2s, kw_ts, m_news, a_olds = [], [], [], [], [], [], []
            for h in heads:
                b_col = tab[:, F_COL + h:F_COL + h + 1]
                b_row = tab_t[F_COL + h:F_COL + h + 1, :]
                li_row = tab_t[I_COL + h:I_COL + h + 1, :]
                b_end = tab_end[:, F_COL + h:F_COL + h + 1]
                m_prev = m_ref[0:1, h:h + 1]
                d_log = jnp.where(causal, b_col - b_row + li_row, neg_inf)
                inter_log = b_col + m_prev
                m_t = jnp.maximum(inter_log, jnp.max(d_log, axis=1, keepdims=True))
                w_intra.append(jnp.exp2(d_log - m_t))
                w_inter.append(jnp.exp2(inter_log - m_t))
                m_ts.append(m_t)
                g_row = b_end - b_row + li_row
                m_loc = jnp.max(g_row, axis=1, keepdims=True)
                m_new = jnp.maximum(b_end + m_prev, m_loc)
                m_news.append(m_new)
                a_olds.append(jnp.exp2(b_end + m_prev - m_new))
                wg_row = jnp.exp2(g_row - m_new)
                kw_ts.append((k_ts[h] * wg_row).astype(BF16))
                wg2s.append(split_hi_lo(wg_row))
            emit()
            c_locs = [_dot(kw_ts[h], v_hs[h]) for h in heads]
            n_locs = [_dot(wg2s[h], jnp.concatenate([k_hs[h], k_hs[h]], axis=0)) for h in heads]
            qks = [qk_raw[h] * w_intra[h] for h in heads]
            nums = [_dot(qks[h].astype(BF16), v_hs[h]) + w_inter[h] * q_c_prev[h] for h in heads]
            emit()
            for h in heads:
                den = (jnp.sum(qks[h], axis=1, keepdims=True)
                       + w_inter[h] * jnp.sum(q_hs[h].astype(F32) * n_prevs[h], axis=1, keepdims=True))
                hh = nums[h] / jnp.maximum(jnp.abs(den), jnp.exp2(-m_ts[h]))
                ms = jnp.mean(hh * hh, axis=-1, keepdims=True)
                hn = hh * lax.rsqrt(ms + NORM_EPS) * ml_nw_ref[:, hcols[h]]
                gate = 0.5 * jnp.tanh(o_ref[rows, hcols[h]].astype(F32)) + 0.5
                hm_ref[rows, hcols[h]] = (gate * hn).astype(BF16)
            for h in heads:
                c_ref[h] = a_olds[h] * c_prevs[h] + c_locs[h]
                n_ref[h] = a_olds[h] * n_prevs[h] + n_locs[h]
                m_ref[0:1, h:h + 1] = m_news[h]
        while pending:
            pending.pop(0)()

    bufs = (conv0_ref, conv1_ref)

    @pl.when(step == 0)
    def _():
        for item in conv_items(bufs[0]):
            item()

    for parity in range(2):
        @pl.when((step > 0) & (step < n_blocks) & (step % 2 == parity))
        def _():
            scans(bufs[1 - parity], conv_items(bufs[parity]))

    @pl.when(step == n_blocks)
    def _():
        scans(bufs[(n_blocks - 1) % 2], [])


def _mixers(main, small, cw_ssd, cb_ssd, cw_qk, cb_qk,
            bias_row, alog_row, dskip_row, ssd_nw, ml_nw, batch, seq):
    tb, L = MIX_TOKENS, SCAN_CHUNK
    spb = seq // tb
    tokens = batch * seq
    assert sum(CONV_PIECES) == tb and len(set(CONV_PIECES)) == 2
    sel_a = _shift_select_matrix(CONV_PIECES[0])
    sel_b = _shift_select_matrix(CONV_PIECES[-1])
    e2 = _head_expand_matrix()
    eye = jnp.eye(ML_HEAD_DIM, dtype=BF16)

    n_blocks = batch * spb

    def ahead(width, idx):
        return pl.BlockSpec((tb, width), lambda s: (jnp.minimum(s, n_blocks - 1), idx))

    def behind(width, idx):
        return pl.BlockSpec((tb, width), lambda s: (jnp.maximum(s - 1, 0), idx))

    def const(shape):
        return pl.BlockSpec(shape, lambda s: tuple(0 for _ in shape))

    in_specs = [
        behind(2 * COL_BLOCK, 0),
        ahead(2 * COL_BLOCK, 1),
        ahead(COL_BLOCK, 4),
        ahead(COL_BLOCK, 5),
        ahead(COL_BLOCK, 6),
        behind(COL_BLOCK, 7),
        behind(COL_BLOCK, 8),
        behind(SMALL_WIDTH, 0),
        const((CONV_TAPS, SSD_XBC)), const((1, SSD_XBC)),
        const((CONV_TAPS, 2 * ML_INNER)), const((1, 2 * ML_INNER)),
        const((1, SMALL_WIDTH)), const((1, SMALL_WIDTH)),
        const((1, SSD_INNER)), const((1, SSD_INNER)), const((1, ML_INNER)),
        const(sel_a.shape), const(sel_b.shape), const(e2.shape), const(eye.shape),
    ]
    out_specs = [behind(SSD_INNER, 0), behind(ML_INNER, 0)]
    scratch = [
        pltpu.VMEM((BF16_ROWS, CONV_WIDTH), BF16),
        pltpu.VMEM((tb, CONV_WIDTH), BF16),
        pltpu.VMEM((tb, CONV_WIDTH), BF16),
        pltpu.VMEM((SSD_STATE, SSD_INNER), F32),
        pltpu.VMEM((ML_HEADS, ML_HEAD_DIM, ML_HEAD_DIM), F32),
        pltpu.VMEM((ML_HEADS, 1, ML_HEAD_DIM), F32),
        pltpu.VMEM((8, LANES), F32),
        pltpu.VMEM((tb, SMALL_WIDTH), F32),
        pltpu.VMEM((tb // L, SMALL_WIDTH, L), F32),
        pltpu.VMEM((tb, SMALL_WIDTH), F32),
    ]
    return pl.pallas_call(
        functools.partial(_mixer_kernel, n_blocks, spb),
        grid=(n_blocks + 1,),
        in_specs=in_specs,
        out_specs=out_specs,
        out_shape=[jax.ShapeDtypeStruct((tokens, SSD_INNER), BF16),
                   jax.ShapeDtypeStruct((tokens, ML_INNER), BF16)],
        scratch_shapes=scratch,
        compiler_params=pltpu.CompilerParams(
            dimension_semantics=("arbitrary",),
            vmem_limit_bytes=VMEM_LIMIT),
        name="mixers",
    )(main, main, main, main, main, main, main, small,
      cw_ssd, cb_ssd, cw_qk, cb_qk,
      bias_row, alog_row, dskip_row, ssd_nw, ml_nw, sel_a, sel_b, e2, eye)


def _rms(x, w):
    ms = jnp.mean(x * x, axis=-1, keepdims=True)
    return x * lax.rsqrt(ms + NORM_EPS) * w


def _merge_kernel(x_ref, y_ref, hm_ref, gs_ref, gm_ref,
                  wbs_ref, wbm_ref, wo_ref, nmw_ref, wup_ref, wdn_ref, nfw_ref, out_ref):
    a = _dot(y_ref[...], wbs_ref[...])
    b = _dot(hm_ref[...], wbm_ref[...])
    mixed = _sigmoid(gs_ref[...].astype(F32)) * a + _sigmoid(gm_ref[...].astype(F32)) * b
    h1 = x_ref[...] + _dot(mixed.astype(BF16), wo_ref[...])
    u = _rms(h1, nmw_ref[...]).astype(BF16)
    up = jnp.maximum(_dot(u, wup_ref[...]), 0.0)
    act = (up * up).astype(BF16)
    h2 = h1 + _dot(act, wdn_ref[...])
    out_ref[...] = _rms(h2, nfw_ref[...])


def _merge(x2, y, hm, main, w_br_ssd, w_br_ml, w_out, norm_mlp_w, w_up, w_down, norm_final_w):
    tokens = x2.shape[0]
    tm = MERGE_TM

    def resident(shape):
        return pl.BlockSpec(shape, lambda i: (0, 0), pipeline_mode=pl.Buffered(1))

    return pl.pallas_call(
        _merge_kernel,
        grid=(tokens // tm,),
        in_specs=[
            pl.BlockSpec((tm, D_MODEL), lambda i: (i, 0)),
            pl.BlockSpec((tm, SSD_INNER), lambda i: (i, 0)),
            pl.BlockSpec((tm, ML_INNER), lambda i: (i, 0)),
            pl.BlockSpec((tm, COL_BLOCK), lambda i: (i, 9)),
            pl.BlockSpec((tm, COL_BLOCK), lambda i: (i, 10)),
            resident((SSD_INNER, D_MODEL)),
            resident((ML_INNER, D_MODEL)),
            resident((D_MODEL, D_MODEL)),
            resident((1, D_MODEL)),
            resident((D_MODEL, D_FF)),
            resident((D_FF, D_MODEL)),
            resident((1, D_MODEL)),
        ],
        out_specs=pl.BlockSpec((tm, D_MODEL), lambda i: (i, 0)),
        out_shape=jax.ShapeDtypeStruct((tokens, D_MODEL), F32),
        compiler_params=pltpu.CompilerParams(
            dimension_semantics=("arbitrary",),
            vmem_limit_bytes=VMEM_LIMIT),
        name="merge_mlp",
    )(x2, y, hm, main, main, w_br_ssd, w_br_ml, w_out, norm_mlp_w, w_up, w_down, norm_final_w)


def kernel(x, norm_mix_w, w_in, conv_ssd_w, conv_ssd_b, dt_bias, a_log, d_skip, ssd_norm_w,
           conv_qk_w, conv_qk_b, i_bias, f_bias, mlstm_norm_w, w_br_ssd, w_br_mlstm, w_out,
           norm_mlp_w, w_up, w_down, norm_final_w):
    batch, seq, _ = x.shape
    x2 = x.reshape(batch * seq, D_MODEL)
    layer = 0

    w_t = jnp.transpose(w_in[layer])
    w_main = _w_prep(w_t)
    o_dt = SSD_INNER + SSD_XBC
    o_i = o_dt + SSD_HEADS + 4 * ML_INNER
    pad = SMALL_WIDTH - SSD_HEADS - 2 * ML_HEADS
    w_small = jnp.concatenate([w_t[o_dt:o_dt + SSD_HEADS], w_t[o_i:o_i + 2 * ML_HEADS],
                               jnp.zeros((pad, D_MODEL), F32)], axis=0)

    main, small = _in_proj(x2, norm_mix_w[layer].reshape(1, D_MODEL), w_main, w_small)

    zeros = jnp.zeros((SMALL_WIDTH - F_COL - ML_HEADS,), F32)
    bias_row = jnp.concatenate([dt_bias[layer], i_bias[layer], f_bias[layer], zeros]).reshape(1, SMALL_WIDTH)
    alog_row = jnp.concatenate([a_log[layer], jnp.zeros((SMALL_WIDTH - SSD_HEADS,), F32)]).reshape(1, SMALL_WIDTH)
    dskip_row = jnp.repeat(d_skip[layer].astype(F32), SSD_HEAD_DIM).reshape(1, SSD_INNER)

    y, hm = _mixers(
        main, small,
        conv_ssd_w[layer], conv_ssd_b[layer].reshape(1, SSD_XBC),
        conv_qk_w[layer], conv_qk_b[layer].reshape(1, 2 * ML_INNER),
        bias_row, alog_row, dskip_row,
        ssd_norm_w[layer].reshape(1, SSD_INNER), mlstm_norm_w[layer].reshape(1, ML_INNER),
        batch, seq)

    out = _merge(x2, y, hm, main,
                 w_br_ssd[layer].astype(BF16), w_br_mlstm[layer].astype(BF16), w_out[layer].astype(BF16),
                 norm_mlp_w[layer].reshape(1, D_MODEL), w_up[layer].astype(BF16), w_down[layer].astype(BF16),
                 norm_final_w.reshape(1, D_MODEL))
    return out.reshape(batch, seq, D_MODEL)
```

```python
import functools

import jax
import jax.numpy as jnp
import numpy as np
from jax import lax
from jax.experimental import pallas as pl
from jax.experimental.pallas import tpu as pltpu

F32 = jnp.float32
BF16 = jnp.bfloat16

LOG2E = 1.4426950408889634

D_MODEL = 1024
NORM_EPS = 1e-5
CONV_TAPS = 4
SSD_INNER = 2048
SSD_HEAD_DIM = 64
SSD_HEADS = 32
SSD_GROUPS = 4
SSD_STATE = 128
SSD_GROUP_WIDTH = SSD_INNER // SSD_GROUPS
SSD_XBC = SSD_INNER + 2 * SSD_GROUPS * SSD_STATE
ML_INNER = 1024
ML_HEADS = 4
ML_HEAD_DIM = 256
D_FF = 4096
IN_PROJ_WIDTH = SSD_INNER + SSD_XBC + SSD_HEADS + 4 * ML_INNER + 2 * ML_HEADS + 2 * D_MODEL

LANES = 128
BF16_ROWS = 16

COL_BLOCK = 1024
N_COL_BLOCKS = 11
MAIN_WIDTH = N_COL_BLOCKS * COL_BLOCK
Q_FIRST_BLOCK, G_FIRST_BLOCK = 5, 9
Z_BLOCKS, O_BLOCK = 2, 8
Q_ROW_OFFSET = SSD_HEADS
G_ROW_OFFSET = SSD_HEADS + 2 * ML_HEADS
SMALL_WIDTH = LANES
DT_COL, I_COL, F_COL = 0, 32, 36
DT_ROW_COPY = 64

CONV_WIDTH = SSD_XBC + 2 * ML_INNER
SCAN_CHUNK = 128
MIX_TOKENS = 256
CONV_PIECES = (112, 112, 32)
CONV_STRIP = 256
PROJ_TM = 2048
PROJ_COLS = 2816
PROJ_ROWS = 512
PROJ_VMEM_LIMIT = 62 * 1024 * 1024
MERGE_TM = 512
HEADS_PER_DOT = 4
VMEM_LIMIT = 56 * 1024 * 1024


def _sigmoid(x):
    return 0.5 * jnp.tanh(0.5 * x) + 0.5


def _softplus(x):
    return jnp.maximum(x, 0.0) + jnp.log1p(jnp.exp(-jnp.abs(x)))


def _dot(a, b):
    return jnp.dot(a, b, preferred_element_type=F32)


def _dot_nt(a, b):
    return lax.dot_general(a, b, (((1,), (1,)), ((), ())), preferred_element_type=F32)


def _w_prep_kernel(w_ref, out_ref):
    j = pl.program_id(0)
    halved = (j < Z_BLOCKS) | (j == O_BLOCK)
    out_ref[...] = (w_ref[...] * jnp.where(halved, 0.5, 1.0)).astype(BF16)


def _w_prep(w_t):
    def src_rows(j):
        off = jnp.where(j < Q_FIRST_BLOCK, 0, jnp.where(j < G_FIRST_BLOCK, Q_ROW_OFFSET, G_ROW_OFFSET))
        return (pl.multiple_of(j * COL_BLOCK + off, 8), 0)

    return pl.pallas_call(
        _w_prep_kernel,
        grid=(N_COL_BLOCKS,),
        in_specs=[pl.BlockSpec((pl.Element(COL_BLOCK), pl.Element(D_MODEL)), src_rows)],
        out_specs=pl.BlockSpec((COL_BLOCK, D_MODEL), lambda j: (j, 0)),
        out_shape=jax.ShapeDtypeStruct((MAIN_WIDTH, D_MODEL), BF16),
        compiler_params=pltpu.CompilerParams(
            dimension_semantics=("arbitrary",),
            vmem_limit_bytes=VMEM_LIMIT),
        name="w_prep",
    )(w_t)


def _in_proj_kernel(x_ref, nw_ref, w_ref, ws_ref, main_ref, small_ref, u_ref):
    j = pl.program_id(1)

    @pl.when(j == 0)
    def _():
        x = x_ref[...]
        ms = jnp.mean(x * x, axis=-1, keepdims=True)
        u = (x * lax.rsqrt(ms + NORM_EPS) * nw_ref[...]).astype(BF16)
        u_ref[...] = u
        small_ref[...] = _dot_nt(u, ws_ref[...].astype(BF16))

    for r0 in range(0, x_ref.shape[0], PROJ_ROWS):
        rows = slice(r0, r0 + PROJ_ROWS)
        main_ref[rows, :] = _dot_nt(u_ref[rows, :], w_ref[...]).astype(BF16)


def _in_proj(x2, norm_w, w_main, w_small):
    tokens = x2.shape[0]
    grid = (tokens // PROJ_TM, MAIN_WIDTH // PROJ_COLS)
    return pl.pallas_call(
        _in_proj_kernel,
        grid=grid,
        in_specs=[
            pl.BlockSpec((PROJ_TM, D_MODEL), lambda i, j: (i, 0)),
            pl.BlockSpec((1, D_MODEL), lambda i, j: (0, 0)),
            pl.BlockSpec((PROJ_COLS, D_MODEL), lambda i, j: (j, 0)),
            pl.BlockSpec((SMALL_WIDTH, D_MODEL), lambda i, j: (0, 0)),
        ],
        out_specs=[
            pl.BlockSpec((PROJ_TM, PROJ_COLS), lambda i, j: (i, j)),
            pl.BlockSpec((PROJ_TM, SMALL_WIDTH), lambda i, j: (i, 0)),
        ],
        out_shape=[
            jax.ShapeDtypeStruct((tokens, MAIN_WIDTH), BF16),
            jax.ShapeDtypeStruct((tokens, SMALL_WIDTH), F32),
        ],
        scratch_shapes=[pltpu.VMEM((PROJ_TM, D_MODEL), BF16)],
        compiler_params=pltpu.CompilerParams(
            dimension_semantics=("arbitrary", "arbitrary"),
            vmem_limit_bytes=PROJ_VMEM_LIMIT),
        name="in_proj",
    )(x2, norm_w, w_main, w_small)


def _shift_select_matrix(p):
    h = BF16_ROWS
    sel = np.zeros((p, CONV_TAPS * (p + h)), np.float32)
    for tap in range(CONV_TAPS):
        for t in range(p):
            sel[t, tap * (p + h) + t + h - (CONV_TAPS - 1 - tap)] = 1.0
    return jnp.asarray(sel, BF16)


def _head_expand_matrix():
    e = np.zeros((2 * LANES, SSD_INNER), np.float32)
    for h in range(SSD_HEADS):
        e[h, h * SSD_HEAD_DIM:(h + 1) * SSD_HEAD_DIM] = 1.0
        e[LANES + h, h * SSD_HEAD_DIM:(h + 1) * SSD_HEAD_DIM] = 1.0
    return jnp.asarray(e, BF16)


def _mixer_kernel(n_blocks, seq_blocks, z_ref, xs_ref, bc_ref, q_ref, k_ref, v_ref, o_ref, sm_ref,
                  cw_ssd_ref, cb_ssd_ref, cw_qk_ref, cb_qk_ref,
                  bias_ref, alog_ref, dskip_ref, ssd_nw_ref, ml_nw_ref, sel_a_ref, sel_b_ref, e2_ref, eye_ref,
                  y_ref, hm_ref,
                  hist_ref, conv0_ref, conv1_ref,
                  s_ref, c_ref, n_ref, m_ref, p_ref, pt_ref, dt_ref):
    tb, L = MIX_TOKENS, SCAN_CHUNK
    n_chunks = tb // L
    hp = HEADS_PER_DOT
    qw = hp * SSD_HEAD_DIM
    gw = SSD_GROUP_WIDTH
    step = pl.program_id(0)
    bc_off, q_off, k_off = SSD_INNER, SSD_XBC, SSD_XBC + ML_INNER

    @pl.when(step % seq_blocks == 0)
    def _():
        hist_ref[...] = jnp.zeros_like(hist_ref)

    @pl.when(step % seq_blocks == 1 % seq_blocks)
    def _():
        s_ref[...] = jnp.zeros_like(s_ref)
        c_ref[...] = jnp.zeros_like(c_ref)
        n_ref[...] = jnp.zeros_like(n_ref)
        m_ref[...] = jnp.zeros_like(m_ref)

    def conv_items(dst_ref):
        h = BF16_ROWS
        sels = {CONV_PIECES[0]: sel_a_ref, CONV_PIECES[-1]: sel_b_ref}
        sources = ((xs_ref, 0, SSD_INNER, cw_ssd_ref, cb_ssd_ref, 0),
                   (bc_ref, bc_off, SSD_XBC - SSD_INNER, cw_ssd_ref, cb_ssd_ref, SSD_INNER),
                   (q_ref, q_off, ML_INNER, cw_qk_ref, cb_qk_ref, 0),
                   (k_ref, k_off, ML_INNER, cw_qk_ref, cb_qk_ref, ML_INNER))

        def strip(src_ref, dst_off, w_ref, b_ref, w_off, c0):
            cols = slice(c0, c0 + CONV_STRIP)
            wcols = slice(w_off + c0, w_off + c0 + CONV_STRIP)
            dcols = slice(dst_off + c0, dst_off + c0 + CONV_STRIP)
            taps = [jnp.broadcast_to(0.5 * w_ref[tap:tap + 1, wcols], (h, CONV_STRIP)).astype(BF16)
                    for tap in range(CONV_TAPS)]
            half_bias = 0.5 * b_ref[:, wcols]
            r0 = 0
            for p in CONV_PIECES:
                if r0 == 0:
                    win = jnp.concatenate([hist_ref[:, dcols], src_ref[0:p, cols]], axis=0)
                else:
                    win = src_ref[r0 - h:r0 + p, cols]
                win = win.reshape((p + h) // h, h, CONV_STRIP)
                scaled = jnp.concatenate([(win * taps[tap][None]).reshape(p + h, CONV_STRIP)
                                          for tap in range(CONV_TAPS)], axis=0)
                g = _dot(sels[p][...], scaled) + half_bias
                dst_ref[r0:r0 + p, dcols] = (g * jnp.tanh(g) + g).astype(BF16)
                r0 += p
            hist_ref[:, dcols] = src_ref[tb - h:tb, cols]

        items = []
        for src_ref, dst_off, width, w_ref, b_ref, w_off in sources:
            for c0 in range(0, width, CONV_STRIP):
                items.append(lambda a=(src_ref, dst_off, w_ref, b_ref, w_off, c0): strip(*a))
        return items

    def scans(cv_ref, pending):
        pending = list(pending)
        n_slots = n_chunks * (SSD_GROUPS + 3)
        per_slot = -(-len(pending) // n_slots)

        def emit():
            for _ in range(per_slot):
                if pending:
                    pending.pop(0)()

        sm = sm_ref[...] + bias_ref[...]
        lane = lax.broadcasted_iota(jnp.int32, (tb, SMALL_WIDTH), 1)
        dt = _softplus(sm)
        log_f = -_softplus(-sm)
        a_row = -jnp.exp(alog_ref[...])
        is_dt = lane < I_COL
        is_i = (lane >= I_COL) & (lane < F_COL)
        is_f = (lane >= F_COL) & (lane < F_COL + ML_HEADS)
        pre = jnp.where(is_dt, dt * a_row, jnp.where(is_f, log_f, 0.0))
        tri = (lax.broadcasted_iota(jnp.int32, (L, L), 1)
               <= lax.broadcasted_iota(jnp.int32, (L, L), 0)).astype(F32)
        cs = jnp.concatenate(
            [jnp.dot(tri, pre[c * L:(c + 1) * L, :], preferred_element_type=F32, precision=lax.Precision.HIGHEST)
             for c in range(n_chunks)], axis=0)
        dt_copy = pltpu.roll(dt, DT_ROW_COPY, axis=1)
        is_dt_copy = (lane >= DT_ROW_COPY) & (lane < DT_ROW_COPY + SSD_HEADS)
        table = jnp.where(is_dt | is_f, cs * LOG2E,
                          jnp.where(is_i, sm * LOG2E, jnp.where(is_dt_copy, dt_copy, 0.0)))
        p_ref[...] = table
        dt_ref[...] = jnp.where(is_dt, dt, 0.0)
        for c in range(n_chunks):
            pt_ref[c] = table[c * L:(c + 1) * L, :].T

        causal = (lax.broadcasted_iota(jnp.int32, (L, L), 0) >= lax.broadcasted_iota(jnp.int32, (L, L), 1))
        dt_lanes = lax.broadcasted_iota(jnp.int32, (L, SMALL_WIDTH), 1) < I_COL
        lane_q = lax.broadcasted_iota(jnp.int32, (1, qw), 1)
        head_masks = [(lane_q >= a * SSD_HEAD_DIM) & (lane_q < (a + 1) * SSD_HEAD_DIM) for a in range(hp)]
        neg_inf = jnp.float32(-jnp.inf)

        def split_hi_lo(v):
            hi = v.astype(BF16)
            lo = (v - hi.astype(F32)).astype(BF16)
            return jnp.concatenate([hi, lo], axis=1)

        for c in range(n_chunks):
            r0 = c * L
            rows = slice(r0, r0 + L)
            tab = p_ref[rows, :]
            tab_end = p_ref[r0 + L - 1:r0 + L, :]
            tab_t = pt_ref[c]
            dt_blk = dt_ref[rows, :]
            ea2 = split_hi_lo(jnp.where(dt_lanes, jnp.exp2(tab), 0.0))
            td2 = split_hi_lo(jnp.where(dt_lanes, jnp.exp2(tab_end - tab) * dt_blk, 0.0))

            for g in range(SSD_GROUPS):
                emit()
                gcols = slice(g * gw, (g + 1) * gw)
                b_mat = cv_ref[rows, bc_off + g * SSD_STATE:bc_off + (g + 1) * SSD_STATE]
                c_mat = cv_ref[rows, bc_off + (SSD_GROUPS + g) * SSD_STATE:
                               bc_off + (SSD_GROUPS + g + 1) * SSD_STATE]
                cb = _dot_nt(c_mat, b_mat)
                ea_g = _dot(ea2, e2_ref[:, gcols])
                td_g = _dot(td2, e2_ref[:, gcols])
                x_g = cv_ref[rows, gcols]
                x_gf = x_g.astype(F32)
                s_g = s_ref[:, gcols]
                inter = _dot(c_mat, s_g.astype(BF16)) * ea_g
                intra_parts = []
                for qd in range(gw // qw):
                    x_q = x_g[:, qd * qw:(qd + 1) * qw]
                    zero_x = jnp.zeros_like(x_q)
                    lhs_parts, rhs_parts = [], []
                    for a in range(hp):
                        h = (g * gw + qd * qw) // SSD_HEAD_DIM + a
                        seg = tab[:, h:h + 1] - tab_t[h:h + 1, :]
                        dec = jnp.exp2(jnp.where(causal, seg, neg_inf))
                        w_mat = cb * dec * tab_t[DT_ROW_COPY + h:DT_ROW_COPY + h + 1, :]
                        lhs_parts.append(w_mat.astype(BF16))
                        rhs_parts.append(jnp.where(head_masks[a], x_q, zero_x))
                    intra_parts.append(_dot(jnp.concatenate(lhs_parts, axis=1),
                                            jnp.concatenate(rhs_parts, axis=0)))
                y_g = jnp.concatenate(intra_parts, axis=1) + inter + dskip_ref[:, gcols] * x_gf
                zh = z_ref[rows, gcols].astype(F32)
                yz = y_g * (zh * jnp.tanh(zh) + zh)
                ms = jnp.mean(yz * yz, axis=-1, keepdims=True)
                y_ref[rows, gcols] = (yz * lax.rsqrt(ms + NORM_EPS) * ssd_nw_ref[:, gcols]).astype(BF16)
                xw = (x_gf * td_g).astype(BF16)
                b_t = _dot_nt(eye_ref[0:SSD_STATE, 0:SSD_STATE], b_mat).astype(BF16)
                s_ref[:, gcols] = s_g * ea_g[L - 1:L, :] + _dot(b_t, xw)

            emit()
            heads = range(ML_HEADS)
            hcols = [slice(h * ML_HEAD_DIM, (h + 1) * ML_HEAD_DIM) for h in heads]
            q_hs = [cv_ref[rows, q_off + h * ML_HEAD_DIM:q_off + (h + 1) * ML_HEAD_DIM]
                    * jnp.asarray(ML_HEAD_DIM ** -0.5, BF16) for h in heads]
            k_hs = [cv_ref[rows, k_off + h * ML_HEAD_DIM:k_off + (h + 1) * ML_HEAD_DIM] for h in heads]
            v_hs = [v_ref[rows, hcols[h]] for h in heads]
            c_prevs = [c_ref[h] for h in heads]
            n_prevs = [n_ref[h] for h in heads]
            qk_raw = [_dot_nt(q_hs[h], k_hs[h]) for h in heads]
            q_c_prev = [_dot(q_hs[h], c_prevs[h].astype(BF16)) for h in heads]
            k_ts = [_dot_nt(eye_ref[...], k_hs[h]) for h in heads]
            w_intra, w_inter, m_ts, wg2s, kw_ts, m_news, a_olds = [], [], [], [], [], [], []
            for h in heads:
                b_col = tab[:, F_COL + h:F_COL + h + 1]
                b_row = tab_t[F_COL + h:F_COL + h + 1, :]
                li_row = tab_t[I_COL + h:I_COL + h + 1, :]
                b_end = tab_end[:, F_COL + h:F_COL + h + 1]
                m_prev = m_ref[0:1, h:h + 1]
                d_log = jnp.where(causal, b_col - b_row + li_row, neg_inf)
                inter_log = b_col + m_prev
                m_t = jnp.maximum(inter_log, jnp.max(d_log, axis=1, keepdims=True))
                w_intra.append(jnp.exp2(d_log - m_t))
                w_inter.append(jnp.exp2(inter_log - m_t))
                m_ts.append(m_t)
                g_row = b_end - b_row + li_row
                m_loc = jnp.max(g_row, axis=1, keepdims=True)
                m_new = jnp.maximum(b_end + m_prev, m_loc)
                m_news.append(m_new)
                a_olds.append(jnp.exp2(b_end + m_prev - m_new))
                wg_row = jnp.exp2(g_row - m_new)
                kw_ts.append((k_ts[h] * wg_row).astype(BF16))
                wg2s.append(split_hi_lo(wg_row))
            emit()
            c_locs = [_dot(kw_ts[h], v_hs[h]) for h in heads]
            n_locs = [_dot(wg2s[h], jnp.concatenate([k_hs[h], k_hs[h]], axis=0)) for h in heads]
            qks = [qk_raw[h] * w_intra[h] for h in heads]
            nums = [_dot(qks[h].astype(BF16), v_hs[h]) + w_inter[h] * q_c_prev[h] for h in heads]
            emit()
            for h in heads:
                den = (jnp.sum(qks[h], axis=1, keepdims=True)
                       + w_inter[h] * jnp.sum(q_hs[h].astype(F32) * n_prevs[h], axis=1, keepdims=True))
                hh = nums[h] / jnp.maximum(jnp.abs(den), jnp.exp2(-m_ts[h]))
                ms = jnp.mean(hh * hh, axis=-1, keepdims=True)
                hn = hh * lax.rsqrt(ms + NORM_EPS) * ml_nw_ref[:, hcols[h]]
                gate = 0.5 * jnp.tanh(o_ref[rows, hcols[h]].astype(F32)) + 0.5
                hm_ref[rows, hcols[h]] = (gate * hn).astype(BF16)
            for h in heads:
                c_ref[h] = a_olds[h] * c_prevs[h] + c_locs[h]
                n_ref[h] = a_olds[h] * n_prevs[h] + n_locs[h]
                m_ref[0:1, h:h + 1] = m_news[h]
        while pending:
            pending.pop(0)()

    bufs = (conv0_ref, conv1_ref)

    @pl.when(step == 0)
    def _():
        for item in conv_items(bufs[0]):
            item()

    for parity in range(2):
        @pl.when((step > 0) & (step < n_blocks) & (step % 2 == parity))
        def _():
            scans(bufs[1 - parity], conv_items(bufs[parity]))

    @pl.when(step == n_blocks)
    def _():
        scans(bufs[(n_blocks - 1) % 2], [])


def _mixers(main, small, cw_ssd, cb_ssd, cw_qk, cb_qk,
            bias_row, alog_row, dskip_row, ssd_nw, ml_nw, batch, seq):
    tb, L = MIX_TOKENS, SCAN_CHUNK
    spb = seq // tb
    tokens = batch * seq
    assert sum(CONV_PIECES) == tb and len(set(CONV_PIECES)) == 2
    sel_a = _shift_select_matrix(CONV_PIECES[0])
    sel_b = _shift_select_matrix(CONV_PIECES[-1])
    e2 = _head_expand_matrix()
    eye = jnp.eye(ML_HEAD_DIM, dtype=BF16)

    n_blocks = batch * spb

    def ahead(width, idx):
        return pl.BlockSpec((tb, width), lambda s: (jnp.minimum(s, n_blocks - 1), idx))

    def behind(width, idx):
        return pl.BlockSpec((tb, width), lambda s: (jnp.maximum(s - 1, 0), idx))

    def const(shape):
        return pl.BlockSpec(shape, lambda s: tuple(0 for _ in shape))

    in_specs = [
        behind(2 * COL_BLOCK, 0),
        ahead(2 * COL_BLOCK, 1),
        ahead(COL_BLOCK, 4),
        ahead(COL_BLOCK, 5),
        ahead(COL_BLOCK, 6),
        behind(COL_BLOCK, 7),
        behind(COL_BLOCK, 8),
        behind(SMALL_WIDTH, 0),
        const((CONV_TAPS, SSD_XBC)), const((1, SSD_XBC)),
        const((CONV_TAPS, 2 * ML_INNER)), const((1, 2 * ML_INNER)),
        const((1, SMALL_WIDTH)), const((1, SMALL_WIDTH)),
        const((1, SSD_INNER)), const((1, SSD_INNER)), const((1, ML_INNER)),
        const(sel_a.shape), const(sel_b.shape), const(e2.shape), const(eye.shape),
    ]
    out_specs = [behind(SSD_INNER, 0), behind(ML_INNER, 0)]
    scratch = [
        pltpu.VMEM((BF16_ROWS, CONV_WIDTH), BF16),
        pltpu.VMEM((tb, CONV_WIDTH), BF16),
        pltpu.VMEM((tb, CONV_WIDTH), BF16),
        pltpu.VMEM((SSD_STATE, SSD_INNER), F32),
        pltpu.VMEM((ML_HEADS, ML_HEAD_DIM, ML_HEAD_DIM), F32),
        pltpu.VMEM((ML_HEADS, 1, ML_HEAD_DIM), F32),
        pltpu.VMEM((8, LANES), F32),
        pltpu.VMEM((tb, SMALL_WIDTH), F32),
        pltpu.VMEM((tb // L, SMALL_WIDTH, L), F32),
        pltpu.VMEM((tb, SMALL_WIDTH), F32),
    ]
    return pl.pallas_call(
        functools.partial(_mixer_kernel, n_blocks, spb),
        grid=(n_blocks + 1,),
        in_specs=in_specs,
        out_specs=out_specs,
        out_shape=[jax.ShapeDtypeStruct((tokens, SSD_INNER), BF16),
                   jax.ShapeDtypeStruct((tokens, ML_INNER), BF16)],
        scratch_shapes=scratch,
        compiler_params=pltpu.CompilerParams(
            dimension_semantics=("arbitrary",),
            vmem_limit_bytes=VMEM_LIMIT),
        name="mixers",
    )(main, main, main, main, main, main, main, small,
      cw_ssd, cb_ssd, cw_qk, cb_qk,
      bias_row, alog_row, dskip_row, ssd_nw, ml_nw, sel_a, sel_b, e2, eye)


def _rms(x, w):
    ms = jnp.mean(x * x, axis=-1, keepdims=True)
    return x * lax.rsqrt(ms + NORM_EPS) * w


def _merge_kernel(x_ref, y_ref, hm_ref, gs_ref, gm_ref,
                  wbs_ref, wbm_ref, wo_ref, nmw_ref, wup_ref, wdn_ref, nfw_ref, out_ref):
    a = _dot(y_ref[...], wbs_ref[...])
    b = _dot(hm_ref[...], wbm_ref[...])
    mixed = _sigmoid(gs_ref[...].astype(F32)) * a + _sigmoid(gm_ref[...].astype(F32)) * b
    h1 = x_ref[...] + _dot(mixed.astype(BF16), wo_ref[...])
    u = _rms(h1, nmw_ref[...]).astype(BF16)
    up = jnp.maximum(_dot(u, wup_ref[...]), 0.0)
    act = (up * up).astype(BF16)
    h2 = h1 + _dot(act, wdn_ref[...])
    out_ref[...] = _rms(h2, nfw_ref[...])


def _merge(x2, y, hm, main, w_br_ssd, w_br_ml, w_out, norm_mlp_w, w_up, w_down, norm_final_w):
    tokens = x2.shape[0]
    tm = MERGE_TM

    def resident(shape):
        return pl.BlockSpec(shape, lambda i: (0, 0), pipeline_mode=pl.Buffered(1))

    return pl.pallas_call(
        _merge_kernel,
        grid=(tokens // tm,),
        in_specs=[
            pl.BlockSpec((tm, D_MODEL), lambda i: (i, 0)),
            pl.BlockSpec((tm, SSD_INNER), lambda i: (i, 0)),
            pl.BlockSpec((tm, ML_INNER), lambda i: (i, 0)),
            pl.BlockSpec((tm, COL_BLOCK), lambda i: (i, 9)),
            pl.BlockSpec((tm, COL_BLOCK), lambda i: (i, 10)),
            resident((SSD_INNER, D_MODEL)),
            resident((ML_INNER, D_MODEL)),
            resident((D_MODEL, D_MODEL)),
            resident((1, D_MODEL)),
            resident((D_MODEL, D_FF)),
            resident((D_FF, D_MODEL)),
            resident((1, D_MODEL)),
        ],
        out_specs=pl.BlockSpec((tm, D_MODEL), lambda i: (i, 0)),
        out_shape=jax.ShapeDtypeStruct((tokens, D_MODEL), F32),
        compiler_params=pltpu.CompilerParams(
            dimension_semantics=("arbitrary",),
            vmem_limit_bytes=VMEM_LIMIT),
        name="merge_mlp",
    )(x2, y, hm, main, main, w_br_ssd, w_br_ml, w_out, norm_mlp_w, w_up, w_down, norm_final_w)


def kernel(x, norm_mix_w, w_in, conv_ssd_w, conv_ssd_b, dt_bias, a_log, d_skip, ssd_norm_w,
           conv_qk_w, conv_qk_b, i_bias, f_bias, mlstm_norm_w, w_br_ssd, w_br_mlstm, w_out,
           norm_mlp_w, w_up, w_down, norm_final_w):
    batch, seq, _ = x.shape
    x2 = x.reshape(batch * seq, D_MODEL)
    layer = 0

    w_t = jnp.transpose(w_in[layer])
    w_main = _w_prep(w_t)
    o_dt = SSD_INNER + SSD_XBC
    o_i = o_dt + SSD_HEADS + 4 * ML_INNER
    pad = SMALL_WIDTH - SSD_HEADS - 2 * ML_HEADS
    w_small = jnp.concatenate([w_t[o_dt:o_dt + SSD_HEADS], w_t[o_i:o_i + 2 * ML_HEADS],
                               jnp.zeros((pad, D_MODEL), F32)], axis=0)

    main, small = _in_proj(x2, norm_mix_w[layer].reshape(1, D_MODEL), w_main, w_small)

    zeros = jnp.zeros((SMALL_WIDTH - F_COL - ML_HEADS,), F32)
    bias_row = jnp.concatenate([dt_bias[layer], i_bias[layer], f_bias[layer], zeros]).reshape(1, SMALL_WIDTH)
    alog_row = jnp.concatenate([a_log[layer], jnp.zeros((SMALL_WIDTH - SSD_HEADS,), F32)]).reshape(1, SMALL_WIDTH)
    dskip_row = jnp.repeat(d_skip[layer].astype(F32), SSD_HEAD_DIM).reshape(1, SSD_INNER)

    y, hm = _mixers(
        main, small,
        conv_ssd_w[layer], conv_ssd_b[layer].reshape(1, SSD_XBC),
        conv_qk_w[layer], conv_qk_b[layer].reshape(1, 2 * ML_INNER),
        bias_row, alog_row, dskip_row,
        ssd_norm_w[layer].reshape(1, SSD_INNER), mlstm_norm_w[layer].reshape(1, ML_INNER),
        batch, seq)

    out = _merge(x2, y, hm, main,
                 w_br_ssd[layer].astype(BF16), w_br_mlstm[layer].astype(BF16), w_out[layer].astype(BF16),
                 norm_mlp_w[layer].reshape(1, D_MODEL), w_up[layer].astype(BF16), w_down[layer].astype(BF16),
                 norm_final_w.reshape(1, D_MODEL))
    return out.reshape(batch, seq, D_MODEL)
```

```python
import functools

import jax
import jax.numpy as jnp
import numpy as np
from jax import lax
from jax.experimental import pallas as pl
from jax.experimental.pallas import tpu as pltpu

F32 = jnp.float32
BF16 = jnp.bfloat16

LOG2E = 1.4426950408889634

D_MODEL = 1024
NORM_EPS = 1e-5
CONV_TAPS = 4
SSD_INNER = 2048
SSD_HEAD_DIM = 64
SSD_HEADS = 32
SSD_GROUPS = 4
SSD_STATE = 128
SSD_GROUP_WIDTH = SSD_INNER // SSD_GROUPS
SSD_XBC = SSD_INNER + 2 * SSD_GROUPS * SSD_STATE
ML_INNER = 1024
ML_HEADS = 4
ML_HEAD_DIM = 256
D_FF = 4096
IN_PROJ_WIDTH = SSD_INNER + SSD_XBC + SSD_HEADS + 4 * ML_INNER + 2 * ML_HEADS + 2 * D_MODEL

LANES = 128
BF16_ROWS = 16

COL_BLOCK = 1024
N_COL_BLOCKS = 11
MAIN_WIDTH = N_COL_BLOCKS * COL_BLOCK
Q_FIRST_BLOCK, G_FIRST_BLOCK = 5, 9
Z_BLOCKS, O_BLOCK = 2, 8
Q_ROW_OFFSET = SSD_HEADS
G_ROW_OFFSET = SSD_HEADS + 2 * ML_HEADS
SMALL_WIDTH = LANES
DT_COL, I_COL, F_COL = 0, 32, 36
DT_ROW_COPY = 64

CONV_WIDTH = SSD_XBC + 2 * ML_INNER
SCAN_CHUNK = 128
MIX_TOKENS = 256
CONV_PIECES = (112, 112, 32)
CONV_STRIP = 256
PROJ_TM = 2048
PROJ_COLS = 2816
PROJ_ROWS = 512
PROJ_VMEM_LIMIT = 62 * 1024 * 1024
MERGE_TM = 512
HEADS_PER_DOT = 2
SSD_HALF = 256
VMEM_LIMIT = 56 * 1024 * 1024


def _sigmoid(x):
    return 0.5 * jnp.tanh(0.5 * x) + 0.5


def _softplus(x):
    return jnp.maximum(x, 0.0) + jnp.log1p(jnp.exp(-jnp.abs(x)))


def _dot(a, b):
    return jnp.dot(a, b, preferred_element_type=F32)


def _dot_nt(a, b):
    return lax.dot_general(a, b, (((1,), (1,)), ((), ())), preferred_element_type=F32)


def _w_prep_kernel(w_ref, out_ref):
    j = pl.program_id(0)
    halved = (j < Z_BLOCKS) | (j == O_BLOCK)
    out_ref[...] = (w_ref[...] * jnp.where(halved, 0.5, 1.0)).astype(BF16)


def _w_prep(w_t):
    def src_rows(j):
        off = jnp.where(j < Q_FIRST_BLOCK, 0, jnp.where(j < G_FIRST_BLOCK, Q_ROW_OFFSET, G_ROW_OFFSET))
        return (pl.multiple_of(j * COL_BLOCK + off, 8), 0)

    return pl.pallas_call(
        _w_prep_kernel,
        grid=(N_COL_BLOCKS,),
        in_specs=[pl.BlockSpec((pl.Element(COL_BLOCK), pl.Element(D_MODEL)), src_rows)],
        out_specs=pl.BlockSpec((COL_BLOCK, D_MODEL), lambda j: (j, 0)),
        out_shape=jax.ShapeDtypeStruct((MAIN_WIDTH, D_MODEL), BF16),
        compiler_params=pltpu.CompilerParams(
            dimension_semantics=("arbitrary",),
            vmem_limit_bytes=VMEM_LIMIT),
        name="w_prep",
    )(w_t)


def _in_proj_kernel(x_ref, nw_ref, w_ref, ws_ref, main_ref, small_ref, u_ref):
    j = pl.program_id(1)
    pieces = [slice(r0, r0 + PROJ_ROWS) for r0 in range(0, x_ref.shape[0], PROJ_ROWS)]

    @pl.when(j == 0)
    def _():
        ws = ws_ref[...].astype(BF16)
        for rows in pieces:
            x = x_ref[rows, :]
            ms = jnp.mean(x * x, axis=-1, keepdims=True)
            u = (x * lax.rsqrt(ms + NORM_EPS) * nw_ref[...]).astype(BF16)
            u_ref[rows, :] = u
            small_ref[rows, :] = _dot_nt(u, ws)
            main_ref[rows, :] = _dot_nt(u, w_ref[...]).astype(BF16)

    @pl.when(j > 0)
    def _():
        for rows in pieces:
            main_ref[rows, :] = _dot_nt(u_ref[rows, :], w_ref[...]).astype(BF16)


def _in_proj(x2, norm_w, w_main, w_small):
    tokens = x2.shape[0]
    grid = (tokens // PROJ_TM, MAIN_WIDTH // PROJ_COLS)
    return pl.pallas_call(
        _in_proj_kernel,
        grid=grid,
        in_specs=[
            pl.BlockSpec((PROJ_TM, D_MODEL), lambda i, j: (i, 0)),
            pl.BlockSpec((1, D_MODEL), lambda i, j: (0, 0)),
            pl.BlockSpec((PROJ_COLS, D_MODEL), lambda i, j: (j, 0)),
            pl.BlockSpec((SMALL_WIDTH, D_MODEL), lambda i, j: (0, 0)),
        ],
        out_specs=[
            pl.BlockSpec((PROJ_TM, PROJ_COLS), lambda i, j: (i, j)),
            pl.BlockSpec((PROJ_TM, SMALL_WIDTH), lambda i, j: (i, 0)),
        ],
        out_shape=[
            jax.ShapeDtypeStruct((tokens, MAIN_WIDTH), BF16),
            jax.ShapeDtypeStruct((tokens, SMALL_WIDTH), F32),
        ],
        scratch_shapes=[pltpu.VMEM((PROJ_TM, D_MODEL), BF16)],
        compiler_params=pltpu.CompilerParams(
            dimension_semantics=("arbitrary", "arbitrary"),
            vmem_limit_bytes=PROJ_VMEM_LIMIT),
        name="in_proj",
    )(x2, norm_w, w_main, w_small)


def _shift_select_matrix(p):
    h = BF16_ROWS
    sel = np.zeros((p, CONV_TAPS * (p + h)), np.float32)
    for tap in range(CONV_TAPS):
        for t in range(p):
            sel[t, tap * (p + h) + t + h - (CONV_TAPS - 1 - tap)] = 1.0
    return jnp.asarray(sel, BF16)


def _head_expand_matrix():
    e = np.zeros((2 * LANES, SSD_INNER), np.float32)
    for h in range(SSD_HEADS):
        e[h, h * SSD_HEAD_DIM:(h + 1) * SSD_HEAD_DIM] = 1.0
        e[LANES + h, h * SSD_HEAD_DIM:(h + 1) * SSD_HEAD_DIM] = 1.0
    return jnp.asarray(e, BF16)


def _mixer_kernel(n_blocks, seq_blocks, z_ref, xs_ref, bc_ref, q_ref, k_ref, v_ref, o_ref, sm_ref,
                  cw_ssd_ref, cb_ssd_ref, cw_qk_ref, cb_qk_ref,
                  bias_ref, alog_ref, dskip_ref, ssd_nw_ref, ml_nw_ref, sel_a_ref, sel_b_ref, e2_ref, eye_ref,
                  y_ref, hm_ref,
                  hist_ref, conv0_ref, conv1_ref,
                  s_ref, c_ref, n_ref, m_ref, p_ref, pt_ref, dt_ref):
    tb, L = MIX_TOKENS, SCAN_CHUNK
    n_chunks = tb // L
    hp = HEADS_PER_DOT
    qw = hp * SSD_HEAD_DIM
    gw = SSD_GROUP_WIDTH
    step = pl.program_id(0)
    bc_off, q_off, k_off = SSD_INNER, SSD_XBC, SSD_XBC + ML_INNER

    @pl.when(step % seq_blocks == 0)
    def _():
        hist_ref[...] = jnp.zeros_like(hist_ref)

    @pl.when(step % seq_blocks == 1 % seq_blocks)
    def _():
        s_ref[...] = jnp.zeros_like(s_ref)
        c_ref[...] = jnp.zeros_like(c_ref)
        n_ref[...] = jnp.zeros_like(n_ref)
        m_ref[...] = jnp.zeros_like(m_ref)

    def conv_items(dst_ref):
        h = BF16_ROWS
        sels = {CONV_PIECES[0]: sel_a_ref, CONV_PIECES[-1]: sel_b_ref}
        sources = ((xs_ref, 0, SSD_INNER, cw_ssd_ref, cb_ssd_ref, 0),
                   (bc_ref, bc_off, SSD_XBC - SSD_INNER, cw_ssd_ref, cb_ssd_ref, SSD_INNER),
                   (q_ref, q_off, ML_INNER, cw_qk_ref, cb_qk_ref, 0),
                   (k_ref, k_off, ML_INNER, cw_qk_ref, cb_qk_ref, ML_INNER))

        def strip(src_ref, dst_off, w_ref, b_ref, w_off, c0):
            cols = slice(c0, c0 + CONV_STRIP)
            wcols = slice(w_off + c0, w_off + c0 + CONV_STRIP)
            dcols = slice(dst_off + c0, dst_off + c0 + CONV_STRIP)
            taps = [jnp.broadcast_to(0.5 * w_ref[tap:tap + 1, wcols], (h, CONV_STRIP)).astype(BF16)
                    for tap in range(CONV_TAPS)]
            half_bias = 0.5 * b_ref[:, wcols]
            r0 = 0
            for p in CONV_PIECES:
                if r0 == 0:
                    win = jnp.concatenate([hist_ref[:, dcols], src_ref[0:p, cols]], axis=0)
                else:
                    win = src_ref[r0 - h:r0 + p, cols]
                win = win.reshape((p + h) // h, h, CONV_STRIP)
                scaled = jnp.concatenate([(win * taps[tap][None]).reshape(p + h, CONV_STRIP)
                                          for tap in range(CONV_TAPS)], axis=0)
                g = _dot(sels[p][...], scaled) + half_bias
                dst_ref[r0:r0 + p, dcols] = (g * jnp.tanh(g) + g).astype(BF16)
                r0 += p
            hist_ref[:, dcols] = src_ref[tb - h:tb, cols]

        items = []
        for src_ref, dst_off, width, w_ref, b_ref, w_off in sources:
            for c0 in range(0, width, CONV_STRIP):
                items.append(lambda a=(src_ref, dst_off, w_ref, b_ref, w_off, c0): strip(*a))
        return items

    def scans(cv_ref, pending):
        pending = list(pending)
        n_slots = n_chunks * (SSD_GROUPS + 3)
        per_slot = -(-len(pending) // n_slots)

        def emit():
            for _ in range(per_slot):
                if pending:
                    pending.pop(0)()

        sm = sm_ref[...] + bias_ref[...]
        lane = lax.broadcasted_iota(jnp.int32, (tb, SMALL_WIDTH), 1)
        dt = _softplus(sm)
        log_f = -_softplus(-sm)
        a_row = -jnp.exp(alog_ref[...])
        is_dt = lane < I_COL
        is_i = (lane >= I_COL) & (lane < F_COL)
        is_f = (lane >= F_COL) & (lane < F_COL + ML_HEADS)
        pre = jnp.where(is_dt, dt * a_row, jnp.where(is_f, log_f, 0.0))
        tri = (lax.broadcasted_iota(jnp.int32, (L, L), 1)
               <= lax.broadcasted_iota(jnp.int32, (L, L), 0)).astype(F32)
        cs = jnp.concatenate(
            [jnp.dot(tri, pre[c * L:(c + 1) * L, :], preferred_element_type=F32, precision=lax.Precision.HIGHEST)
             for c in range(n_chunks)], axis=0)
        dt_copy = pltpu.roll(dt, DT_ROW_COPY, axis=1)
        is_dt_copy = (lane >= DT_ROW_COPY) & (lane < DT_ROW_COPY + SSD_HEADS)
        table = jnp.where(is_dt | is_f, cs * LOG2E,
                          jnp.where(is_i, sm * LOG2E, jnp.where(is_dt_copy, dt_copy, 0.0)))
        p_ref[...] = table
        dt_ref[...] = jnp.where(is_dt, dt, 0.0)
        for c in range(n_chunks):
            pt_ref[c] = table[c * L:(c + 1) * L, :].T

        causal = (lax.broadcasted_iota(jnp.int32, (L, L), 0) >= lax.broadcasted_iota(jnp.int32, (L, L), 1))
        dt_lanes = lax.broadcasted_iota(jnp.int32, (L, SMALL_WIDTH), 1) < I_COL
        lane_q = lax.broadcasted_iota(jnp.int32, (1, qw), 1)
        head_masks = [(lane_q >= a * SSD_HEAD_DIM) & (lane_q < (a + 1) * SSD_HEAD_DIM) for a in range(hp)]
        neg_inf = jnp.float32(-jnp.inf)

        def split_hi_lo(v):
            hi = v.astype(BF16)
            lo = (v - hi.astype(F32)).astype(BF16)
            return jnp.concatenate([hi, lo], axis=1)

        for c in range(n_chunks):
            r0 = c * L
            rows = slice(r0, r0 + L)
            tab = p_ref[rows, :]
            tab_end = p_ref[r0 + L - 1:r0 + L, :]
            tab_t = pt_ref[c]
            dt_blk = dt_ref[rows, :]
            ea2 = split_hi_lo(jnp.where(dt_lanes, jnp.exp2(tab), 0.0))
            td2 = split_hi_lo(jnp.where(dt_lanes, jnp.exp2(tab_end - tab) * dt_blk, 0.0))

            for g in range(SSD_GROUPS):
                emit()
                gcols = slice(g * gw, (g + 1) * gw)
                b_mat = cv_ref[rows, bc_off + g * SSD_STATE:bc_off + (g + 1) * SSD_STATE]
                c_mat = cv_ref[rows, bc_off + (SSD_GROUPS + g) * SSD_STATE:
                               bc_off + (SSD_GROUPS + g + 1) * SSD_STATE]
                cb = _dot_nt(c_mat, b_mat)
                b_t = _dot_nt(eye_ref[0:SSD_STATE, 0:SSD_STATE], b_mat).astype(BF16)
                yz_halves, sq_sum = [], None
                for c0 in range(g * gw, (g + 1) * gw, SSD_HALF):
                    hcols_ = slice(c0, c0 + SSD_HALF)
                    ea_h = _dot(ea2, e2_ref[:, hcols_])
                    td_h = _dot(td2, e2_ref[:, hcols_])
                    x_h = cv_ref[rows, hcols_]
                    x_hf = x_h.astype(F32)
                    s_h = s_ref[:, hcols_]
                    inter = _dot(c_mat, s_h.astype(BF16)) * ea_h
                    intra_parts = []
                    for qd in range(SSD_HALF // qw):
                        x_q = x_h[:, qd * qw:(qd + 1) * qw]
                        zero_x = jnp.zeros_like(x_q)
                        lhs_parts, rhs_parts = [], []
                        for a in range(hp):
                            h = (c0 + qd * qw) // SSD_HEAD_DIM + a
                            seg = tab[:, h:h + 1] - tab_t[h:h + 1, :]
                            dec = jnp.exp2(jnp.where(causal, seg, neg_inf))
                            w_mat = cb * dec * tab_t[DT_ROW_COPY + h:DT_ROW_COPY + h + 1, :]
                            lhs_parts.append(w_mat.astype(BF16))
                            rhs_parts.append(jnp.where(head_masks[a], x_q, zero_x))
                        intra_parts.append(_dot(jnp.concatenate(lhs_parts, axis=1),
                                                jnp.concatenate(rhs_parts, axis=0)))
                    y_h = jnp.concatenate(intra_parts, axis=1) + inter + dskip_ref[:, hcols_] * x_hf
                    zh = z_ref[rows, hcols_].astype(F32)
                    yz = y_h * (zh * jnp.tanh(zh) + zh)
                    part = jnp.sum(yz * yz, axis=-1, keepdims=True)
                    sq_sum = part if sq_sum is None else sq_sum + part
                    yz_halves.append((hcols_, yz))
                    xw = (x_hf * td_h).astype(BF16)
                    s_ref[:, hcols_] = s_h * ea_h[L - 1:L, :] + _dot(b_t, xw)
                scale = lax.rsqrt(sq_sum * (1.0 / gw) + NORM_EPS)
                for hcols_, yz in yz_halves:
                    y_ref[rows, hcols_] = (yz * scale * ssd_nw_ref[:, hcols_]).astype(BF16)

            emit()
            heads = range(ML_HEADS)
            hcols = [slice(h * ML_HEAD_DIM, (h + 1) * ML_HEAD_DIM) for h in heads]
            q_hs = [cv_ref[rows, q_off + h * ML_HEAD_DIM:q_off + (h + 1) * ML_HEAD_DIM]
                    * jnp.asarray(ML_HEAD_DIM ** -0.5, BF16) for h in heads]
            k_hs = [cv_ref[rows, k_off + h * ML_HEAD_DIM:k_off + (h + 1) * ML_HEAD_DIM] for h in heads]
            v_hs = [v_ref[rows, hcols[h]] for h in heads]
            c_prevs = [c_ref[h] for h in heads]
            n_prevs = [n_ref[h] for h in heads]
            qk_raw = [_dot_nt(q_hs[h], k_hs[h]) for h in heads]
            q_c_prev = [_dot(q_hs[h], c_prevs[h].astype(BF16)) for h in heads]
            k_ts = [_dot_nt(eye_ref[...], k_hs[h]) for h in heads]
            w_intra, w_inter, m_ts, wg2s, kw_ts, m_news, a_olds = [], [], [], [], [], [], []
            for h in heads:
                b_col = tab[:, F_COL + h:F_COL + h + 1]
                b_row = tab_t[F_COL + h:F_COL + h + 1, :]
                li_row = tab_t[I_COL + h:I_COL + h + 1, :]
                b_end = tab_end[:, F_COL + h:F_COL + h + 1]
                m_prev = m_ref[0:1, h:h + 1]
                d_log = jnp.where(causal, b_col - b_row + li_row, neg_inf)
                inter_log = b_col + m_prev
                m_t = jnp.maximum(inter_log, jnp.max(d_log, axis=1, keepdims=True))
                w_intra.append(jnp.exp2(d_log - m_t))
                w_inter.append(jnp.exp2(inter_log - m_t))
                m_ts.append(m_t)
                g_row = b_end - b_row + li_row
                m_loc = jnp.max(g_row, axis=1, keepdims=True)
                m_new = jnp.maximum(b_end + m_prev, m_loc)
                m_news.append(m_new)
                a_olds.append(jnp.exp2(b_end + m_prev - m_new))
                wg_row = jnp.exp2(g_row - m_new)
                kw_ts.append((k_ts[h] * wg_row).astype(BF16))
                wg2s.append(split_hi_lo(wg_row))
            emit()
            c_locs = [_dot(kw_ts[h], v_hs[h]) for h in heads]
            n_locs = [_dot(wg2s[h], jnp.concatenate([k_hs[h], k_hs[h]], axis=0)) for h in heads]
            qks = [qk_raw[h] * w_intra[h] for h in heads]
            nums = [_dot(qks[h].astype(BF16), v_hs[h]) + w_inter[h] * q_c_prev[h] for h in heads]
            emit()
            for h in heads:
                den = (jnp.sum(qks[h], axis=1, keepdims=True)
                       + w_inter[h] * jnp.sum(q_hs[h].astype(F32) * n_prevs[h], axis=1, keepdims=True))
                hh = nums[h] / jnp.maximum(jnp.abs(den), jnp.exp2(-m_ts[h]))
                ms = jnp.mean(hh * hh, axis=-1, keepdims=True)
                hn = hh * lax.rsqrt(ms + NORM_EPS) * ml_nw_ref[:, hcols[h]]
                gate = 0.5 * jnp.tanh(o_ref[rows, hcols[h]].astype(F32)) + 0.5
                hm_ref[rows, hcols[h]] = (gate * hn).astype(BF16)
            for h in heads:
                c_ref[h] = a_olds[h] * c_prevs[h] + c_locs[h]
                n_ref[h] = a_olds[h] * n_prevs[h] + n_locs[h]
                m_ref[0:1, h:h + 1] = m_news[h]
        while pending:
            pending.pop(0)()

    bufs = (conv0_ref, conv1_ref)

    @pl.when(step == 0)
    def _():
        for item in conv_items(bufs[0]):
            item()

    for parity in range(2):
        @pl.when((step > 0) & (step < n_blocks) & (step % 2 == parity))
        def _():
            scans(bufs[1 - parity], conv_items(bufs[parity]))

    @pl.when(step == n_blocks)
    def _():
        scans(bufs[(n_blocks - 1) % 2], [])


def _mixers(main, small, cw_ssd, cb_ssd, cw_qk, cb_qk,
            bias_row, alog_row, dskip_row, ssd_nw, ml_nw, batch, seq):
    tb, L = MIX_TOKENS, SCAN_CHUNK
    spb = seq // tb
    tokens = batch * seq
    assert sum(CONV_PIECES) == tb and len(set(CONV_PIECES)) == 2
    sel_a = _shift_select_matrix(CONV_PIECES[0])
    sel_b = _shift_select_matrix(CONV_PIECES[-1])
    e2 = _head_expand_matrix()
    eye = jnp.eye(ML_HEAD_DIM, dtype=BF16)

    n_blocks = batch * spb

    def ahead(width, idx):
        return pl.BlockSpec((tb, width), lambda s: (jnp.minimum(s, n_blocks - 1), idx))

    def behind(width, idx):
        return pl.BlockSpec((tb, width), lambda s: (jnp.maximum(s - 1, 0), idx))

    def const(shape):
        return pl.BlockSpec(shape, lambda s: tuple(0 for _ in shape))

    in_specs = [
        behind(2 * COL_BLOCK, 0),
        ahead(2 * COL_BLOCK, 1),
        ahead(COL_BLOCK, 4),
        ahead(COL_BLOCK, 5),
        ahead(COL_BLOCK, 6),
        behind(COL_BLOCK, 7),
        behind(COL_BLOCK, 8),
        behind(SMALL_WIDTH, 0),
        const((CONV_TAPS, SSD_XBC)), const((1, SSD_XBC)),
        const((CONV_TAPS, 2 * ML_INNER)), const((1, 2 * ML_INNER)),
        const((1, SMALL_WIDTH)), const((1, SMALL_WIDTH)),
        const((1, SSD_INNER)), const((1, SSD_INNER)), const((1, ML_INNER)),
        const(sel_a.shape), const(sel_b.shape), const(e2.shape), const(eye.shape),
    ]
    out_specs = [behind(SSD_INNER, 0), behind(ML_INNER, 0)]
    scratch = [
        pltpu.VMEM((BF16_ROWS, CONV_WIDTH), BF16),
        pltpu.VMEM((tb, CONV_WIDTH), BF16),
        pltpu.VMEM((tb, CONV_WIDTH), BF16),
        pltpu.VMEM((SSD_STATE, SSD_INNER), F32),
        pltpu.VMEM((ML_HEADS, ML_HEAD_DIM, ML_HEAD_DIM), F32),
        pltpu.VMEM((ML_HEADS, 1, ML_HEAD_DIM), F32),
        pltpu.VMEM((8, LANES), F32),
        pltpu.VMEM((tb, SMALL_WIDTH), F32),
        pltpu.VMEM((tb // L, SMALL_WIDTH, L), F32),
        pltpu.VMEM((tb, SMALL_WIDTH), F32),
    ]
    return pl.pallas_call(
        functools.partial(_mixer_kernel, n_blocks, spb),
        grid=(n_blocks + 1,),
        in_specs=in_specs,
        out_specs=out_specs,
        out_shape=[jax.ShapeDtypeStruct((tokens, SSD_INNER), BF16),
                   jax.ShapeDtypeStruct((tokens, ML_INNER), BF16)],
        scratch_shapes=scratch,
        compiler_params=pltpu.CompilerParams(
            dimension_semantics=("arbitrary",),
            vmem_limit_bytes=VMEM_LIMIT),
        name="mixers",
    )(main, main, main, main, main, main, main, small,
      cw_ssd, cb_ssd, cw_qk, cb_qk,
      bias_row, alog_row, dskip_row, ssd_nw, ml_nw, sel_a, sel_b, e2, eye)


def _rms(x, w):
    ms = jnp.mean(x * x, axis=-1, keepdims=True)
    return x * lax.rsqrt(ms + NORM_EPS) * w


def _merge_kernel(x_ref, y_ref, hm_ref, gs_ref, gm_ref,
                  wbs_ref, wbm_ref, wo_ref, nmw_ref, wup_ref, wdn_ref, nfw_ref, out_ref):
    a = _dot(y_ref[...], wbs_ref[...])
    b = _dot(hm_ref[...], wbm_ref[...])
    mixed = _sigmoid(gs_ref[...].astype(F32)) * a + _sigmoid(gm_ref[...].astype(F32)) * b
    h1 = x_ref[...] + _dot(mixed.astype(BF16), wo_ref[...])
    u = _rms(h1, nmw_ref[...]).astype(BF16)
    up = jnp.maximum(_dot(u, wup_ref[...]), 0.0)
    act = (up * up).astype(BF16)
    h2 = h1 + _dot(act, wdn_ref[...])
    out_ref[...] = _rms(h2, nfw_ref[...])


def _merge(x2, y, hm, main, w_br_ssd, w_br_ml, w_out, norm_mlp_w, w_up, w_down, norm_final_w):
    tokens = x2.shape[0]
    tm = MERGE_TM

    def resident(shape):
        return pl.BlockSpec(shape, lambda i: (0, 0), pipeline_mode=pl.Buffered(1))

    return pl.pallas_call(
        _merge_kernel,
        grid=(tokens // tm,),
        in_specs=[
            pl.BlockSpec((tm, D_MODEL), lambda i: (i, 0)),
            pl.BlockSpec((tm, SSD_INNER), lambda i: (i, 0)),
            pl.BlockSpec((tm, ML_INNER), lambda i: (i, 0)),
            pl.BlockSpec((tm, COL_BLOCK), lambda i: (i, 9)),
            pl.BlockSpec((tm, COL_BLOCK), lambda i: (i, 10)),
            resident((SSD_INNER, D_MODEL)),
            resident((ML_INNER, D_MODEL)),
            resident((D_MODEL, D_MODEL)),
            resident((1, D_MODEL)),
            resident((D_MODEL, D_FF)),
            resident((D_FF, D_MODEL)),
            resident((1, D_MODEL)),
        ],
        out_specs=pl.BlockSpec((tm, D_MODEL), lambda i: (i, 0)),
        out_shape=jax.ShapeDtypeStruct((tokens, D_MODEL), F32),
        compiler_params=pltpu.CompilerParams(
            dimension_semantics=("arbitrary",),
            vmem_limit_bytes=VMEM_LIMIT),
        name="merge_mlp",
    )(x2, y, hm, main, main, w_br_ssd, w_br_ml, w_out, norm_mlp_w, w_up, w_down, norm_final_w)


def kernel(x, norm_mix_w, w_in, conv_ssd_w, conv_ssd_b, dt_bias, a_log, d_skip, ssd_norm_w,
           conv_qk_w, conv_qk_b, i_bias, f_bias, mlstm_norm_w, w_br_ssd, w_br_mlstm, w_out,
           norm_mlp_w, w_up, w_down, norm_final_w):
    batch, seq, _ = x.shape
    x2 = x.reshape(batch * seq, D_MODEL)
    layer = 0

    w_t = jnp.transpose(w_in[layer])
    w_main = _w_prep(w_t)
    o_dt = SSD_INNER + SSD_XBC
    o_i = o_dt + SSD_HEADS + 4 * ML_INNER
    pad = SMALL_WIDTH - SSD_HEADS - 2 * ML_HEADS
    w_small = jnp.concatenate([w_t[o_dt:o_dt + SSD_HEADS], w_t[o_i:o_i + 2 * ML_HEADS],
                               jnp.zeros((pad, D_MODEL), F32)], axis=0)

    main, small = _in_proj(x2, norm_mix_w[layer].reshape(1, D_MODEL), w_main, w_small)

    zeros = jnp.zeros((SMALL_WIDTH - F_COL - ML_HEADS,), F32)
    bias_row = jnp.concatenate([dt_bias[layer], i_bias[layer], f_bias[layer], zeros]).reshape(1, SMALL_WIDTH)
    alog_row = jnp.concatenate([a_log[layer], jnp.zeros((SMALL_WIDTH - SSD_HEADS,), F32)]).reshape(1, SMALL_WIDTH)
    dskip_row = jnp.repeat(d_skip[layer].astype(F32), SSD_HEAD_DIM).reshape(1, SSD_INNER)

    y, hm = _mixers(
        main, small,
        conv_ssd_w[layer], conv_ssd_b[layer].reshape(1, SSD_XBC),
        conv_qk_w[layer], conv_qk_b[layer].reshape(1, 2 * ML_INNER),
        bias_row, alog_row, dskip_row,
        ssd_norm_w[layer].reshape(1, SSD_INNER), mlstm_norm_w[layer].reshape(1, ML_INNER),
        batch, seq)

    out = _merge(x2, y, hm, main,
                 w_br_ssd[layer].astype(BF16), w_br_mlstm[layer].astype(BF16), w_out[layer].astype(BF16),
                 norm_mlp_w[layer].reshape(1, D_MODEL), w_up[layer].astype(BF16), w_down[layer].astype(BF16),
                 norm_final_w.reshape(1, D_MODEL))
    return out.reshape(batch, seq, D_MODEL)
```

```python
import functools

import jax
import jax.numpy as jnp
import numpy as np
from jax import lax
from jax.experimental import pallas as pl
from jax.experimental.pallas import tpu as pltpu

F32 = jnp.float32
BF16 = jnp.bfloat16

LOG2E = 1.4426950408889634

D_MODEL = 1024
NORM_EPS = 1e-5
CONV_TAPS = 4
SSD_INNER = 2048
SSD_HEAD_DIM = 64
SSD_HEADS = 32
SSD_GROUPS = 4
SSD_STATE = 128
SSD_GROUP_WIDTH = SSD_INNER // SSD_GROUPS
SSD_XBC = SSD_INNER + 2 * SSD_GROUPS * SSD_STATE
ML_INNER = 1024
ML_HEADS = 4
ML_HEAD_DIM = 256
D_FF = 4096
IN_PROJ_WIDTH = SSD_INNER + SSD_XBC + SSD_HEADS + 4 * ML_INNER + 2 * ML_HEADS + 2 * D_MODEL

LANES = 128
BF16_ROWS = 16

COL_BLOCK = 1024
N_COL_BLOCKS = 11
MAIN_WIDTH = N_COL_BLOCKS * COL_BLOCK
Q_FIRST_BLOCK, G_FIRST_BLOCK = 5, 9
Z_BLOCKS, O_BLOCK = 2, 8
Q_ROW_OFFSET = SSD_HEADS
G_ROW_OFFSET = SSD_HEADS + 2 * ML_HEADS
SMALL_WIDTH = LANES
DT_COL, I_COL, F_COL = 0, 32, 36
DT_ROW_COPY = 64

CONV_WIDTH = SSD_XBC + 2 * ML_INNER
SCAN_CHUNK = 128
MIX_TOKENS = 256
CONV_PIECES = (112, 112, 32)
CONV_STRIP = 256
PROJ_TM = 2048
PROJ_COLS = 2816
PROJ_ROWS = 512
PROJ_VMEM_LIMIT = 62 * 1024 * 1024
MERGE_TM = 512
HEADS_PER_DOT = 2
VMEM_LIMIT = 56 * 1024 * 1024


def _sigmoid(x):
    return 0.5 * jnp.tanh(0.5 * x) + 0.5


def _softplus(x):
    return jnp.maximum(x, 0.0) + jnp.log1p(jnp.exp(-jnp.abs(x)))


def _dot(a, b):
    return jnp.dot(a, b, preferred_element_type=F32)


def _dot_nt(a, b):
    return lax.dot_general(a, b, (((1,), (1,)), ((), ())), preferred_element_type=F32)


def _w_prep_kernel(w_ref, out_ref):
    j = pl.program_id(0)
    halved = (j < Z_BLOCKS) | (j == O_BLOCK)
    out_ref[...] = (w_ref[...] * jnp.where(halved, 0.5, 1.0)).astype(BF16)


def _w_prep(w_t):
    def src_rows(j):
        off = jnp.where(j < Q_FIRST_BLOCK, 0, jnp.where(j < G_FIRST_BLOCK, Q_ROW_OFFSET, G_ROW_OFFSET))
        return (pl.multiple_of(j * COL_BLOCK + off, 8), 0)

    return pl.pallas_call(
        _w_prep_kernel,
        grid=(N_COL_BLOCKS,),
        in_specs=[pl.BlockSpec((pl.Element(COL_BLOCK), pl.Element(D_MODEL)), src_rows)],
        out_specs=pl.BlockSpec((COL_BLOCK, D_MODEL), lambda j: (j, 0)),
        out_shape=jax.ShapeDtypeStruct((MAIN_WIDTH, D_MODEL), BF16),
        compiler_params=pltpu.CompilerParams(
            dimension_semantics=("arbitrary",),
            vmem_limit_bytes=VMEM_LIMIT),
        name="w_prep",
    )(w_t)


def _in_proj_kernel(x_ref, nw_ref, w_ref, ws_ref, main_ref, small_ref, u_ref):
    j = pl.program_id(1)
    pieces = [slice(r0, r0 + PROJ_ROWS) for r0 in range(0, x_ref.shape[0], PROJ_ROWS)]

    @pl.when(j == 0)
    def _():
        ws = ws_ref[...].astype(BF16)
        for rows in pieces:
            x = x_ref[rows, :]
            ms = jnp.mean(x * x, axis=-1, keepdims=True)
            u = (x * lax.rsqrt(ms + NORM_EPS) * nw_ref[...]).astype(BF16)
            u_ref[rows, :] = u
            small_ref[rows, :] = _dot_nt(u, ws)
            main_ref[rows, :] = _dot_nt(u, w_ref[...]).astype(BF16)

    @pl.when(j > 0)
    def _():
        for rows in pieces:
            main_ref[rows, :] = _dot_nt(u_ref[rows, :], w_ref[...]).astype(BF16)


def _in_proj(x2, norm_w, w_main, w_small):
    tokens = x2.shape[0]
    grid = (tokens // PROJ_TM, MAIN_WIDTH // PROJ_COLS)
    return pl.pallas_call(
        _in_proj_kernel,
        grid=grid,
        in_specs=[
            pl.BlockSpec((PROJ_TM, D_MODEL), lambda i, j: (i, 0)),
            pl.BlockSpec((1, D_MODEL), lambda i, j: (0, 0)),
            pl.BlockSpec((PROJ_COLS, D_MODEL), lambda i, j: (j, 0)),
            pl.BlockSpec((SMALL_WIDTH, D_MODEL), lambda i, j: (0, 0)),
        ],
        out_specs=[
            pl.BlockSpec((PROJ_TM, PROJ_COLS), lambda i, j: (i, j)),
            pl.BlockSpec((PROJ_TM, SMALL_WIDTH), lambda i, j: (i, 0)),
        ],
        out_shape=[
            jax.ShapeDtypeStruct((tokens, MAIN_WIDTH), BF16),
            jax.ShapeDtypeStruct((tokens, SMALL_WIDTH), F32),
        ],
        scratch_shapes=[pltpu.VMEM((PROJ_TM, D_MODEL), BF16)],
        compiler_params=pltpu.CompilerParams(
            dimension_semantics=("arbitrary", "arbitrary"),
            vmem_limit_bytes=PROJ_VMEM_LIMIT),
        name="in_proj",
    )(x2, norm_w, w_main, w_small)


def _shift_select_matrix(p):
    h = BF16_ROWS
    sel = np.zeros((p, CONV_TAPS * (p + h)), np.float32)
    for tap in range(CONV_TAPS):
        for t in range(p):
            sel[t, tap * (p + h) + t + h - (CONV_TAPS - 1 - tap)] = 1.0
    return jnp.asarray(sel, BF16)


def _head_expand_matrix():
    e = np.zeros((2 * LANES, SSD_INNER), np.float32)
    for h in range(SSD_HEADS):
        e[h, h * SSD_HEAD_DIM:(h + 1) * SSD_HEAD_DIM] = 1.0
        e[LANES + h, h * SSD_HEAD_DIM:(h + 1) * SSD_HEAD_DIM] = 1.0
    return jnp.asarray(e, BF16)


def _mixer_kernel(n_blocks, seq_blocks, z_ref, xs_ref, bc_ref, q_ref, k_ref, v_ref, o_ref, sm_ref,
                  cw_ssd_ref, cb_ssd_ref, cw_qk_ref, cb_qk_ref,
                  bias_ref, alog_ref, dskip_ref, ssd_nw_ref, ml_nw_ref, sel_a_ref, sel_b_ref, e2_ref, eye_ref,
                  y_ref, hm_ref,
                  hist_ref, conv0_ref, conv1_ref,
                  s_ref, c_ref, n_ref, m_ref, p_ref, pt_ref, dt_ref):
    tb, L = MIX_TOKENS, SCAN_CHUNK
    n_chunks = tb // L
    hp = HEADS_PER_DOT
    qw = hp * SSD_HEAD_DIM
    gw = SSD_GROUP_WIDTH
    step = pl.program_id(0)
    bc_off, q_off, k_off = SSD_INNER, SSD_XBC, SSD_XBC + ML_INNER

    @pl.when(step % seq_blocks == 0)
    def _():
        hist_ref[...] = jnp.zeros_like(hist_ref)

    @pl.when(step % seq_blocks == 1 % seq_blocks)
    def _():
        s_ref[...] = jnp.zeros_like(s_ref)
        c_ref[...] = jnp.zeros_like(c_ref)
        n_ref[...] = jnp.zeros_like(n_ref)
        m_ref[...] = jnp.zeros_like(m_ref)

    def conv_items(dst_ref):
        h = BF16_ROWS
        sels = {CONV_PIECES[0]: sel_a_ref, CONV_PIECES[-1]: sel_b_ref}
        sources = ((xs_ref, 0, SSD_INNER, cw_ssd_ref, cb_ssd_ref, 0),
                   (bc_ref, bc_off, SSD_XBC - SSD_INNER, cw_ssd_ref, cb_ssd_ref, SSD_INNER),
                   (q_ref, q_off, ML_INNER, cw_qk_ref, cb_qk_ref, 0),
                   (k_ref, k_off, ML_INNER, cw_qk_ref, cb_qk_ref, ML_INNER))

        def strip(src_ref, dst_off, w_ref, b_ref, w_off, c0):
            cols = slice(c0, c0 + CONV_STRIP)
            wcols = slice(w_off + c0, w_off + c0 + CONV_STRIP)
            dcols = slice(dst_off + c0, dst_off + c0 + CONV_STRIP)
            taps = [jnp.broadcast_to(0.5 * w_ref[tap:tap + 1, wcols], (h, CONV_STRIP)).astype(BF16)
                    for tap in range(CONV_TAPS)]
            half_bias = 0.5 * b_ref[:, wcols]
            r0 = 0
            for p in CONV_PIECES:
                if r0 == 0:
                    win = jnp.concatenate([hist_ref[:, dcols], src_ref[0:p, cols]], axis=0)
                else:
                    win = src_ref[r0 - h:r0 + p, cols]
                win = win.reshape((p + h) // h, h, CONV_STRIP)
                scaled = jnp.concatenate([(win * taps[tap][None]).reshape(p + h, CONV_STRIP)
                                          for tap in range(CONV_TAPS)], axis=0)
                g = _dot(sels[p][...], scaled) + half_bias
                dst_ref[r0:r0 + p, dcols] = (g * jnp.tanh(g) + g).astype(BF16)
                r0 += p
            hist_ref[:, dcols] = src_ref[tb - h:tb, cols]

        items = []
        for src_ref, dst_off, width, w_ref, b_ref, w_off in sources:
            for c0 in range(0, width, CONV_STRIP):
                items.append(lambda a=(src_ref, dst_off, w_ref, b_ref, w_off, c0): strip(*a))
        return items

    def scans(cv_ref, pending):
        pending = list(pending)
        n_slots = n_chunks * (SSD_GROUPS + 3)
        per_slot = -(-len(pending) // n_slots)

        def emit():
            for _ in range(per_slot):
                if pending:
                    pending.pop(0)()

        sm = sm_ref[...] + bias_ref[...]
        lane = lax.broadcasted_iota(jnp.int32, (tb, SMALL_WIDTH), 1)
        dt = _softplus(sm)
        log_f = -_softplus(-sm)
        a_row = -jnp.exp(alog_ref[...])
        is_dt = lane < I_COL
        is_i = (lane >= I_COL) & (lane < F_COL)
        is_f = (lane >= F_COL) & (lane < F_COL + ML_HEADS)
        pre = jnp.where(is_dt, dt * a_row, jnp.where(is_f, log_f, 0.0))
        tri = (lax.broadcasted_iota(jnp.int32, (L, L), 1)
               <= lax.broadcasted_iota(jnp.int32, (L, L), 0)).astype(BF16)
        pre_hi = pre.astype(BF16)
        rem = pre - pre_hi.astype(F32)
        pre_mid = rem.astype(BF16)
        pre_lo = (rem - pre_mid.astype(F32)).astype(BF16)
        pre3 = jnp.concatenate([pre_hi, pre_mid, pre_lo], axis=1)
        cs3 = jnp.concatenate([_dot(tri, pre3[c * L:(c + 1) * L, :]) for c in range(n_chunks)], axis=0)
        cs = (cs3[:, :SMALL_WIDTH] + cs3[:, SMALL_WIDTH:2 * SMALL_WIDTH]) + cs3[:, 2 * SMALL_WIDTH:]
        dt_copy = pltpu.roll(dt, DT_ROW_COPY, axis=1)
        is_dt_copy = (lane >= DT_ROW_COPY) & (lane < DT_ROW_COPY + SSD_HEADS)
        table = jnp.where(is_dt | is_f, cs * LOG2E,
                          jnp.where(is_i, sm * LOG2E, jnp.where(is_dt_copy, dt_copy, 0.0)))
        p_ref[...] = table
        dt_ref[...] = jnp.where(is_dt, dt, 0.0)
        for c in range(n_chunks):
            pt_ref[c] = table[c * L:(c + 1) * L, :].T

        causal = (lax.broadcasted_iota(jnp.int32, (L, L), 0) >= lax.broadcasted_iota(jnp.int32, (L, L), 1))
        dt_lanes = lax.broadcasted_iota(jnp.int32, (L, SMALL_WIDTH), 1) < I_COL
        lane_q = lax.broadcasted_iota(jnp.int32, (1, qw), 1)
        head_masks = [(lane_q >= a * SSD_HEAD_DIM) & (lane_q < (a + 1) * SSD_HEAD_DIM) for a in range(hp)]
        neg_inf = jnp.float32(-jnp.inf)

        def split_hi_lo(v):
            hi = v.astype(BF16)
            lo = (v - hi.astype(F32)).astype(BF16)
            return jnp.concatenate([hi, lo], axis=1)

        for c in range(n_chunks):
            r0 = c * L
            rows = slice(r0, r0 + L)
            tab = p_ref[rows, :]
            tab_end = p_ref[r0 + L - 1:r0 + L, :]
            tab_t = pt_ref[c]
            dt_blk = dt_ref[rows, :]
            ea2 = split_hi_lo(jnp.where(dt_lanes, jnp.exp2(tab), 0.0))
            td2 = split_hi_lo(jnp.where(dt_lanes, jnp.exp2(tab_end - tab) * dt_blk, 0.0))

            for g in range(SSD_GROUPS):
                emit()
                gcols = slice(g * gw, (g + 1) * gw)
                b_mat = cv_ref[rows, bc_off + g * SSD_STATE:bc_off + (g + 1) * SSD_STATE]
                c_mat = cv_ref[rows, bc_off + (SSD_GROUPS + g) * SSD_STATE:
                               bc_off + (SSD_GROUPS + g + 1) * SSD_STATE]
                cb = _dot_nt(c_mat, b_mat)
                ea_g = _dot(ea2, e2_ref[:, gcols])
                td_g = _dot(td2, e2_ref[:, gcols])
                x_g = cv_ref[rows, gcols]
                x_gf = x_g.astype(F32)
                s_g = s_ref[:, gcols]
                inter = _dot(c_mat, s_g.astype(BF16)) * ea_g
                intra_parts = []
                for qd in range(gw // qw):
                    x_q = x_g[:, qd * qw:(qd + 1) * qw]
                    zero_x = jnp.zeros_like(x_q)
                    lhs_parts, rhs_parts = [], []
                    for a in range(hp):
                        h = (g * gw + qd * qw) // SSD_HEAD_DIM + a
                        seg = tab[:, h:h + 1] - tab_t[h:h + 1, :]
                        dec = jnp.exp2(jnp.where(causal, seg, neg_inf))
                        w_mat = cb * dec * tab_t[DT_ROW_COPY + h:DT_ROW_COPY + h + 1, :]
                        lhs_parts.append(w_mat.astype(BF16))
                        rhs_parts.append(jnp.where(head_masks[a], x_q, zero_x))
                    intra_parts.append(_dot(jnp.concatenate(lhs_parts, axis=1),
                                            jnp.concatenate(rhs_parts, axis=0)))
                y_g = jnp.concatenate(intra_parts, axis=1) + inter + dskip_ref[:, gcols] * x_gf
                zh = z_ref[rows, gcols].astype(F32)
                yz = y_g * (zh * jnp.tanh(zh) + zh)
                ms = jnp.mean(yz * yz, axis=-1, keepdims=True)
                y_ref[rows, gcols] = (yz * lax.rsqrt(ms + NORM_EPS) * ssd_nw_ref[:, gcols]).astype(BF16)
                xw = (x_gf * td_g).astype(BF16)
                b_t = _dot_nt(eye_ref[0:SSD_STATE, 0:SSD_STATE], b_mat).astype(BF16)
                s_ref[:, gcols] = s_g * ea_g[L - 1:L, :] + _dot(b_t, xw)

            emit()
            heads = range(ML_HEADS)
            hcols = [slice(h * ML_HEAD_DIM, (h + 1) * ML_HEAD_DIM) for h in heads]
            q_hs = [cv_ref[rows, q_off + h * ML_HEAD_DIM:q_off + (h + 1) * ML_HEAD_DIM]
                    * jnp.asarray(ML_HEAD_DIM ** -0.5, BF16) for h in heads]
            k_hs = [cv_ref[rows, k_off + h * ML_HEAD_DIM:k_off + (h + 1) * ML_HEAD_DIM] for h in heads]
            v_hs = [v_ref[rows, hcols[h]] for h in heads]
            c_prevs = [c_ref[h] for h in heads]
            n_prevs = [n_ref[h] for h in heads]
            qk_raw = [_dot_nt(q_hs[h], k_hs[h]) for h in heads]
            q_c_prev = [_dot(q_hs[h], c_prevs[h].astype(BF16)) for h in heads]
            k_ts = [_dot_nt(eye_ref[...], k_hs[h]) for h in heads]
            w_intra, w_inter, m_ts, wg2s, kw_ts, m_news, a_olds = [], [], [], [], [], [], []
            for h in heads:
                b_col = tab[:, F_COL + h:F_COL + h + 1]
                b_row = tab_t[F_COL + h:F_COL + h + 1, :]
                li_row = tab_t[I_COL + h:I_COL + h + 1, :]
                b_end = tab_end[:, F_COL + h:F_COL + h + 1]
                m_prev = m_ref[0:1, h:h + 1]
                d_log = jnp.where(causal, b_col - b_row + li_row, neg_inf)
                inter_log = b_col + m_prev
                m_t = jnp.maximum(inter_log, jnp.max(d_log, axis=1, keepdims=True))
                w_intra.append(jnp.exp2(d_log - m_t))
                w_inter.append(jnp.exp2(inter_log - m_t))
                m_ts.append(m_t)
                g_row = b_end - b_row + li_row
                m_loc = jnp.max(g_row, axis=1, keepdims=True)
                m_new = jnp.maximum(b_end + m_prev, m_loc)
                m_news.append(m_new)
                a_olds.append(jnp.exp2(b_end + m_prev - m_new))
                wg_row = jnp.exp2(g_row - m_new)
                kw_ts.append((k_ts[h] * wg_row).astype(BF16))
                wg2s.append(split_hi_lo(wg_row))
            emit()
            c_locs = [_dot(kw_ts[h], v_hs[h]) for h in heads]
            n_locs = [_dot(wg2s[h], jnp.concatenate([k_hs[h], k_hs[h]], axis=0)) for h in heads]
            qks = [qk_raw[h] * w_intra[h] for h in heads]
            nums = [_dot(qks[h].astype(BF16), v_hs[h]) + w_inter[h] * q_c_prev[h] for h in heads]
            emit()
            for h in heads:
                den = (jnp.sum(qks[h], axis=1, keepdims=True)
                       + w_inter[h] * jnp.sum(q_hs[h].astype(F32) * n_prevs[h], axis=1, keepdims=True))
                hh = nums[h] / jnp.maximum(jnp.abs(den), jnp.exp2(-m_ts[h]))
                ms = jnp.mean(hh * hh, axis=-1, keepdims=True)
                hn = hh * lax.rsqrt(ms + NORM_EPS) * ml_nw_ref[:, hcols[h]]
                gate = 0.5 * jnp.tanh(o_ref[rows, hcols[h]].astype(F32)) + 0.5
                hm_ref[rows, hcols[h]] = (gate * hn).astype(BF16)
            for h in heads:
                c_ref[h] = a_olds[h] * c_prevs[h] + c_locs[h]
                n_ref[h] = a_olds[h] * n_prevs[h] + n_locs[h]
                m_ref[0:1, h:h + 1] = m_news[h]
        while pending:
            pending.pop(0)()

    bufs = (conv0_ref, conv1_ref)

    @pl.when(step == 0)
    def _():
        for item in conv_items(bufs[0]):
            item()

    for parity in range(2):
        @pl.when((step > 0) & (step < n_blocks) & (step % 2 == parity))
        def _():
            scans(bufs[1 - parity], conv_items(bufs[parity]))

    @pl.when(step == n_blocks)
    def _():
        scans(bufs[(n_blocks - 1) % 2], [])


def _mixers(main, small, cw_ssd, cb_ssd, cw_qk, cb_qk,
            bias_row, alog_row, dskip_row, ssd_nw, ml_nw, batch, seq):
    tb, L = MIX_TOKENS, SCAN_CHUNK
    spb = seq // tb
    tokens = batch * seq
    assert sum(CONV_PIECES) == tb and len(set(CONV_PIECES)) == 2
    sel_a = _shift_select_matrix(CONV_PIECES[0])
    sel_b = _shift_select_matrix(CONV_PIECES[-1])
    e2 = _head_expand_matrix()
    eye = jnp.eye(ML_HEAD_DIM, dtype=BF16)

    n_blocks = batch * spb

    def ahead(width, idx):
        return pl.BlockSpec((tb, width), lambda s: (jnp.minimum(s, n_blocks - 1), idx))

    def behind(width, idx):
        return pl.BlockSpec((tb, width), lambda s: (jnp.maximum(s - 1, 0), idx))

    def const(shape):
        return pl.BlockSpec(shape, lambda s: tuple(0 for _ in shape))

    in_specs = [
        behind(2 * COL_BLOCK, 0),
        ahead(2 * COL_BLOCK, 1),
        ahead(COL_BLOCK, 4),
        ahead(COL_BLOCK, 5),
        ahead(COL_BLOCK, 6),
        behind(COL_BLOCK, 7),
        behind(COL_BLOCK, 8),
        behind(SMALL_WIDTH, 0),
        const((CONV_TAPS, SSD_XBC)), const((1, SSD_XBC)),
        const((CONV_TAPS, 2 * ML_INNER)), const((1, 2 * ML_INNER)),
        const((1, SMALL_WIDTH)), const((1, SMALL_WIDTH)),
        const((1, SSD_INNER)), const((1, SSD_INNER)), const((1, ML_INNER)),
        const(sel_a.shape), const(sel_b.shape), const(e2.shape), const(eye.shape),
    ]
    out_specs = [behind(SSD_INNER, 0), behind(ML_INNER, 0)]
    scratch = [
        pltpu.VMEM((BF16_ROWS, CONV_WIDTH), BF16),
        pltpu.VMEM((tb, CONV_WIDTH), BF16),
        pltpu.VMEM((tb, CONV_WIDTH), BF16),
        pltpu.VMEM((SSD_STATE, SSD_INNER), F32),
        pltpu.VMEM((ML_HEADS, ML_HEAD_DIM, ML_HEAD_DIM), F32),
        pltpu.VMEM((ML_HEADS, 1, ML_HEAD_DIM), F32),
        pltpu.VMEM((8, LANES), F32),
        pltpu.VMEM((tb, SMALL_WIDTH), F32),
        pltpu.VMEM((tb // L, SMALL_WIDTH, L), F32),
        pltpu.VMEM((tb, SMALL_WIDTH), F32),
    ]
    return pl.pallas_call(
        functools.partial(_mixer_kernel, n_blocks, spb),
        grid=(n_blocks + 1,),
        in_specs=in_specs,
        out_specs=out_specs,
        out_shape=[jax.ShapeDtypeStruct((tokens, SSD_INNER), BF16),
                   jax.ShapeDtypeStruct((tokens, ML_INNER), BF16)],
        scratch_shapes=scratch,
        compiler_params=pltpu.CompilerParams(
            dimension_semantics=("arbitrary",),
            vmem_limit_bytes=VMEM_LIMIT),
        name="mixers",
    )(main, main, main, main, main, main, main, small,
      cw_ssd, cb_ssd, cw_qk, cb_qk,
      bias_row, alog_row, dskip_row, ssd_nw, ml_nw, sel_a, sel_b, e2, eye)


def _rms(x, w):
    ms = jnp.mean(x * x, axis=-1, keepdims=True)
    return x * lax.rsqrt(ms + NORM_EPS) * w


def _merge_kernel(x_ref, y_ref, hm_ref, gs_ref, gm_ref,
                  wbs_ref, wbm_ref, wo_ref, nmw_ref, wup_ref, wdn_ref, nfw_ref, out_ref):
    a = _dot(y_ref[...], wbs_ref[...])
    b = _dot(hm_ref[...], wbm_ref[...])
    mixed = _sigmoid(gs_ref[...].astype(F32)) * a + _sigmoid(gm_ref[...].astype(F32)) * b
    h1 = x_ref[...] + _dot(mixed.astype(BF16), wo_ref[...])
    u = _rms(h1, nmw_ref[...]).astype(BF16)
    up = jnp.maximum(_dot(u, wup_ref[...]), 0.0)
    act = (up * up).astype(BF16)
    h2 = h1 + _dot(act, wdn_ref[...])
    out_ref[...] = _rms(h2, nfw_ref[...])


def _merge(x2, y, hm, main, w_br_ssd, w_br_ml, w_out, norm_mlp_w, w_up, w_down, norm_final_w):
    tokens = x2.shape[0]
    tm = MERGE_TM

    def resident(shape):
        return pl.BlockSpec(shape, lambda i: (0, 0), pipeline_mode=pl.Buffered(1))

    return pl.pallas_call(
        _merge_kernel,
        grid=(tokens // tm,),
        in_specs=[
            pl.BlockSpec((tm, D_MODEL), lambda i: (i, 0)),
            pl.BlockSpec((tm, SSD_INNER), lambda i: (i, 0)),
            pl.BlockSpec((tm, ML_INNER), lambda i: (i, 0)),
            pl.BlockSpec((tm, COL_BLOCK), lambda i: (i, 9)),
            pl.BlockSpec((tm, COL_BLOCK), lambda i: (i, 10)),
            resident((SSD_INNER, D_MODEL)),
            resident((ML_INNER, D_MODEL)),
            resident((D_MODEL, D_MODEL)),
            resident((1, D_MODEL)),
            resident((D_MODEL, D_FF)),
            resident((D_FF, D_MODEL)),
            resident((1, D_MODEL)),
        ],
        out_specs=pl.BlockSpec((tm, D_MODEL), lambda i: (i, 0)),
        out_shape=jax.ShapeDtypeStruct((tokens, D_MODEL), F32),
        compiler_params=pltpu.CompilerParams(
            dimension_semantics=("arbitrary",),
            vmem_limit_bytes=VMEM_LIMIT),
        name="merge_mlp",
    )(x2, y, hm, main, main, w_br_ssd, w_br_ml, w_out, norm_mlp_w, w_up, w_down, norm_final_w)


def kernel(x, norm_mix_w, w_in, conv_ssd_w, conv_ssd_b, dt_bias, a_log, d_skip, ssd_norm_w,
           conv_qk_w, conv_qk_b, i_bias, f_bias, mlstm_norm_w, w_br_ssd, w_br_mlstm, w_out,
           norm_mlp_w, w_up, w_down, norm_final_w):
    batch, seq, _ = x.shape
    x2 = x.reshape(batch * seq, D_MODEL)
    layer = 0

    w_t = jnp.transpose(w_in[layer])
    w_main = _w_prep(w_t)
    o_dt = SSD_INNER + SSD_XBC
    o_i = o_dt + SSD_HEADS + 4 * ML_INNER
    pad = SMALL_WIDTH - SSD_HEADS - 2 * ML_HEADS
    w_small = jnp.concatenate([w_t[o_dt:o_dt + SSD_HEADS], w_t[o_i:o_i + 2 * ML_HEADS],
                               jnp.zeros((pad, D_MODEL), F32)], axis=0)

    main, small = _in_proj(x2, norm_mix_w[layer].reshape(1, D_MODEL), w_main, w_small)

    zeros = jnp.zeros((SMALL_WIDTH - F_COL - ML_HEADS,), F32)
    bias_row = jnp.concatenate([dt_bias[layer], i_bias[layer], f_bias[layer], zeros]).reshape(1, SMALL_WIDTH)
    alog_row = jnp.concatenate([a_log[layer], jnp.zeros((SMALL_WIDTH - SSD_HEADS,), F32)]).reshape(1, SMALL_WIDTH)
    dskip_row = jnp.repeat(d_skip[layer].astype(F32), SSD_HEAD_DIM).reshape(1, SSD_INNER)

    y, hm = _mixers(
        main, small,
        conv_ssd_w[layer], conv_ssd_b[layer].reshape(1, SSD_XBC),
        conv_qk_w[layer], conv_qk_b[layer].reshape(1, 2 * ML_INNER),
        bias_row, alog_row, dskip_row,
        ssd_norm_w[layer].reshape(1, SSD_INNER), mlstm_norm_w[layer].reshape(1, ML_INNER),
        batch, seq)

    out = _merge(x2, y, hm, main,
                 w_br_ssd[layer].astype(BF16), w_br_mlstm[layer].astype(BF16), w_out[layer].astype(BF16),
                 norm_mlp_w[layer].reshape(1, D_MODEL), w_up[layer].astype(BF16), w_down[layer].astype(BF16),
                 norm_final_w.reshape(1, D_MODEL))
    return out.reshape(batch, seq, D_MODEL)
```
